```python
import math
import jax, jax.numpy as jnp
from jax import lax
import numpy as np

D_MODEL = 2048
BATCH = 8
SEQ = 4096
DEPTH = 2
DEC_BATCH = 8
DEC_SEQ = 64
PAST_LEN = 1024

CHUNK = 64
EPS = 1e-6
NEG = -1e30
HEAD_DIM = 128
A_HEADS = 8
A_KV_HEADS = 2
GQA = A_HEADS // A_KV_HEADS
IDX_HEADS = 8
IDX_DIM = 64
TOPK_MAX = 256
Q_BLOCK = 128
ROPE_THETA = 500000.0
ROT_FRAC = 4
ATTN_SCALE = HEAD_DIM ** -0.5
IDX_SCALE = (IDX_HEADS * IDX_DIM) ** -0.5
B_HEADS = 4
B_DK = 128
B_DV = 128
C_HEADS = 4
C_DK = 64
C_DV = 128
CONV_W = 4
CONV_CH = 2 * C_HEADS * C_DK
A_WIDTH = A_HEADS * HEAD_DIM
B_WIDTH = B_HEADS * B_DV
C_WIDTH = C_HEADS * C_DV
MIX_WIDTH = A_WIDTH + B_WIDTH + C_WIDTH
D_FF = 5632
N_EXPERTS = 8
TOP_K = 2
D_FF_EXPERT = 5632
MOE_BLOCK = 128
N_DENSE = (DEPTH + 1) // 2
N_MOE = DEPTH // 2

IN_SPLITS = (
    ("a_q", A_HEADS * HEAD_DIM), ("a_k", A_KV_HEADS * HEAD_DIM), ("a_v", A_KV_HEADS * HEAD_DIM),
    ("i_q", IDX_HEADS * IDX_DIM), ("i_w", IDX_HEADS), ("i_k", IDX_DIM),
    ("b_q", B_HEADS * B_DK), ("b_f", B_HEADS * B_DK), ("b_i", B_HEADS * B_DV), ("b_g", B_WIDTH),
    ("c_qk", CONV_CH), ("c_v", C_WIDTH), ("c_i", C_HEADS), ("c_f", C_HEADS), ("c_o", C_WIDTH),
)
IN_WIDTH = sum(w for _, w in IN_SPLITS)

kernel_name = "hybrid_dsa_hgrn2_mlstm_streaming_step"

F32 = jnp.float32


def split_cols(z):
    out, off = {}, 0
    for name, w in IN_SPLITS:
        out[name] = z[..., off:off + w]
        off += w
    return out


def rmsnorm(x, g):
    xf = x.astype(F32)
    y = xf * lax.rsqrt(jnp.mean(xf * xf, axis=-1, keepdims=True) + EPS)
    return (y * g.astype(F32)).astype(x.dtype)


def partial_rope(x, pos):
    d = x.shape[-1]
    rot = d // ROT_FRAC
    half = rot // 2
    inv = jnp.exp(-math.log(ROPE_THETA) * 2.0 * jnp.arange(half, dtype=F32) / rot)
    ang = pos.astype(F32)[:, None] * inv
    ang = ang.reshape((pos.shape[0],) + (1,) * (x.ndim - 3) + (half,))
    cos, sin = jnp.cos(ang), jnp.sin(ang)
    xf = x.astype(F32)
    x1, x2, rest = xf[..., :half], xf[..., half:rot], xf[..., rot:]
    return jnp.concatenate([x1 * cos - x2 * sin, x2 * cos + x1 * sin, rest], axis=-1).astype(x.dtype)


def sparse_attention(q, k, v, qi, wi, ki, q_pos, k_pos):
    Bx, Tq = q.shape[0], q.shape[1]
    L = k.shape[1]
    topk = min(TOPK_MAX, L // 4)
    blk = min(Q_BLOCK, Tq)
    nb = Tq // blk
    k_chunk = k_pos // CHUNK
    gather = jax.vmap(lambda arr, idx: arr[idx])

    def to_blocks(a):
        return jnp.moveaxis(a.reshape((Bx, nb, blk) + a.shape[2:]), 1, 0)

    def one_block(args):
        qb, qib, wib, qpb = args
        rel = jax.nn.relu(jnp.einsum('bthd,bsd->bths', qib, ki).astype(F32))
        score = jnp.einsum('bths,bth->bts', rel, wib.astype(F32) * IDX_SCALE)
        admiss = k_chunk[None, :] <= (qpb // CHUNK)[:, None]
        score = jnp.where(admiss[None], score, NEG)
        top_val, top_idx = lax.top_k(score, topk)
        valid = top_val > 0.5 * NEG
        kg = gather(k, top_idx)
        vg = gather(v, top_idx)
        qg = qb.reshape(Bx, blk, A_KV_HEADS, GQA, HEAD_DIM)
        logits = jnp.einsum('bthgd,btkhd->bthgk', qg, kg).astype(F32) * ATTN_SCALE
        logits = jnp.where(valid[:, :, None, None, :], logits, NEG)
        p = jax.nn.softmax(logits, axis=-1)
        o = jnp.einsum('bthgk,btkhd->bthgd', p.astype(v.dtype), vg)
        return o.reshape(Bx, blk, A_HEADS * HEAD_DIM)

    out = lax.map(one_block, (to_blocks(q), to_blocks(qi), to_blocks(wi), q_pos.reshape(nb, blk)))
    return jnp.moveaxis(out, 0, 1).reshape(Bx, Tq, A_HEADS * HEAD_DIM)


def hgrn_chunk(S0, q, k, v, logf):
    b = jnp.cumsum(logf, axis=1)
    n = q.shape[1]
    causal = jnp.tril(jnp.ones((n, n), dtype=bool))
    diff = b[:, :, None] - b[:, None, :]
    decay = jnp.exp(jnp.where(causal[None, :, :, None, None], diff, NEG))
    A = jnp.einsum('bthd,btshd,bshd->bths', q, decay, k)
    o = jnp.einsum('bths,bshv->bthv', A, v) + jnp.einsum('bthd,bhdv->bthv', q * jnp.exp(b), S0)
    b_last = b[:, -1]
    S1 = jnp.exp(b_last)[..., None] * S0 + jnp.einsum('bshd,bshv->bhdv', k * jnp.exp(b_last[:, None] - b), v)
    return S1, o


def mlstm_chunk(state, q, k, v, ig, logf):
    C0, n0, m0 = state
    b = jnp.cumsum(logf, axis=1)
    n = q.shape[1]
    causal = jnp.tril(jnp.ones((n, n), dtype=bool))
    D = b[:, :, None, :] - b[:, None, :, :] + ig[:, None, :, :]
    D = jnp.where(causal[None, :, :, None], D, NEG)
    inter = b + m0[:, None, :]
    m = jnp.maximum(inter, jnp.max(D, axis=2))
    wD = jnp.exp(D - m[:, :, None, :])
    wI = jnp.exp(inter - m)
    qk = jnp.einsum('bthd,bshd->btsh', q, k) * wD
    num = jnp.einsum('btsh,bshv->bthv', qk, v) + wI[..., None] * jnp.einsum('bthd,bhdv->bthv', q, C0)
    den = jnp.sum(qk, axis=2) + wI * jnp.einsum('bthd,bhd->bth', q, n0)
    h = num / jnp.maximum(jnp.abs(den), jnp.exp(-m))[..., None]
    mT = m[:, -1]
    wS = jnp.exp(b[:, -1:, :] - b + ig - mT[:, None, :])
    wC = jnp.exp(b[:, -1] + m0 - mT)
    C1 = wC[..., None, None] * C0 + jnp.einsum('bsh,bshd,bshv->bhdv', wS, k, v)
    n1 = wC[..., None] * n0 + jnp.einsum('bsh,bshd->bhd', wS, k)
    return (C1, n1, mT), h


def chunked_scan(fn, state, arrays):
    T = arrays[0].shape[1]
    if T <= CHUNK:
        return fn(state, *arrays)
    nc = T // CHUNK
    xs = tuple(jnp.moveaxis(a.reshape((a.shape[0], nc, CHUNK) + a.shape[2:]), 1, 0) for a in arrays)
    state, out = lax.scan(lambda s, x: fn(s, *x), state, xs)
    out = jnp.moveaxis(out, 0, 1)
    return state, out.reshape((out.shape[0], T) + out.shape[3:])


def causal_conv(x, state, w, bias):
    T = x.shape[1]
    xp = jnp.concatenate([state.astype(x.dtype), x], axis=1)
    y = bias + sum(w[j] * xp[:, j:j + T] for j in range(CONV_W))
    return jax.nn.silu(y), xp[:, T:]


def token_mixer(xn, pos, W, l, lb, past):
    Bx, T, _ = xn.shape
    dt = xn.dtype
    z = split_cols(jnp.einsum('btd,dn->btn', xn, W['w_in'][l]))
    q = partial_rope(rmsnorm(z['a_q'].reshape(Bx, T, A_HEADS, HEAD_DIM), W['q_norm'][l]), pos)
    k = partial_rope(rmsnorm(z['a_k'].reshape(Bx, T, A_KV_HEADS, HEAD_DIM), W['k_norm'][l]), pos)
    v = z['a_v'].reshape(Bx, T, A_KV_HEADS, HEAD_DIM)
    qi = partial_rope(z['i_q'].reshape(Bx, T, IDX_HEADS, IDX_DIM), pos)
    ki = partial_rope(z['i_k'], pos)
    wi = z['i_w']
    if past is None:
        k_all, v_all, ki_all, k_pos = k, v, ki, pos
    else:
        k_all = jnp.concatenate([past['k'][l].astype(dt), k], axis=1)
        v_all = jnp.concatenate([past['v'][l].astype(dt), v], axis=1)
        ki_all = jnp.concatenate([past['ki'][l].astype(dt), ki], axis=1)
        k_pos = jnp.arange(k_all.shape[1], dtype=jnp.int32)
    o_a = sparse_attention(q, k_all, v_all, qi, wi, ki_all, pos, k_pos)
    lbh = lb.reshape(B_HEADS, B_DK)
    fz = z['b_f'].astype(F32).reshape(Bx, T, B_HEADS, B_DK)
    f = lbh + (1.0 - lbh) * jax.nn.sigmoid(fz)
    logf = jnp.log(jnp.maximum(f, 1e-30))
    kb = 1.0 - f
    qb = jax.nn.silu(z['b_q'].astype(F32)).reshape(Bx, T, B_HEADS, B_DK)
    vb = z['b_i'].astype(F32).reshape(Bx, T, B_HEADS, B_DV)
    S0 = jnp.zeros((Bx, B_HEADS, B_DK, B_DV), F32) if past is None else past['hgrn'][l].astype(F32)
    S1, ob = chunked_scan(hgrn_chunk, S0, (qb, kb, vb, logf))
    gb = jax.nn.silu(z['b_g'].astype(F32)).reshape(Bx, T, B_HEADS, B_DV)
    ob = (rmsnorm(ob, W['hgrn_norm'][l]) * gb).reshape(Bx, T, B_WIDTH).astype(dt)
    conv0 = jnp.zeros((Bx, CONV_W - 1, CONV_CH), dt) if past is None else past['conv'][l]
    qk_c, conv1 = causal_conv(z['c_qk'], conv0, W['conv_w'][l], W['conv_b'][l])
    qc = qk_c[..., :CONV_CH // 2].astype(F32).reshape(Bx, T, C_HEADS, C_DK)
    kc = qk_c[..., CONV_CH // 2:].astype(F32).reshape(Bx, T, C_HEADS, C_DK) * (C_DK ** -0.5)
    vc = z['c_v'].astype(F32).reshape(Bx, T, C_HEADS, C_DV)
    ig = z['c_i'].astype(F32) + W['ig_b'][l].astype(F32)
    logfc = jax.nn.log_sigmoid(z['c_f'].astype(F32) + W['fg_b'][l].astype(F32))
    if past is None:
        st0 = (jnp.zeros((Bx, C_HEADS, C_DK, C_DV), F32), jnp.zeros((Bx, C_HEADS, C_DK), F32),
               jnp.zeros((Bx, C_HEADS), F32))
    else:
        st0 = (past['C'][l].astype(F32), past['n'][l].astype(F32), past['m'][l].astype(F32))
    (C1, n1, m1), hc = chunked_scan(mlstm_chunk, st0, (qc, kc, vc, ig, logfc))
    oc = jax.nn.sigmoid(z['c_o'].astype(F32)).reshape(Bx, T, C_HEADS, C_DV)
    hc = (rmsnorm(hc, W['mlstm_norm'][l]) * oc).reshape(Bx, T, C_WIDTH).astype(dt)
    mix = jnp.einsum('btm,md->btd', jnp.concatenate([o_a, ob, hc], axis=-1), W['w_out'][l])
    new_state = dict(k=k, v=v, ki=ki, hgrn=S1.astype(dt), C=C1.astype(dt), n=n1.astype(dt),
                     m=m1.astype(dt), conv=conv1)
    return mix, new_state


def swiglu(x, wg, wu, wd):
    return jnp.einsum('btf,fd->btd', jax.nn.silu(jnp.einsum('btd,df->btf', x, wg)) * jnp.einsum('btd,df->btf', x, wu), wd)


def moe_ffn(x, w_router, b_router, wg, wu, wd):
    Bx, T, D = x.shape
    N = Bx * T
    xf = x.reshape(N, D)
    logits = jnp.einsum('nd,de->ne', xf, w_router).astype(F32) + b_router.astype(F32)
    top_l, top_e = lax.top_k(logits, TOP_K)
    gates = jax.nn.softmax(top_l, axis=-1)
    NK = N * TOP_K
    flat_e = top_e.reshape(NK)
    flat_tok = jnp.repeat(jnp.arange(N, dtype=jnp.int32), TOP_K)
    flat_g = gates.reshape(NK)
    order = jnp.argsort(flat_e)
    se = flat_e[order]
    counts = jnp.bincount(flat_e, length=N_EXPERTS)
    padded = (counts + MOE_BLOCK - 1) // MOE_BLOCK * MOE_BLOCK
    pad_end = jnp.cumsum(padded)
    pad_start = pad_end - padded
    start = jnp.cumsum(counts) - counts
    dest = pad_start[se] + jnp.arange(NK) - start[se]
    nblk = -(-(NK + N_EXPERTS * (MOE_BLOCK - 1)) // MOE_BLOCK)
    P = nblk * MOE_BLOCK
    buf_tok = jnp.full((P,), N, jnp.int32).at[dest].set(flat_tok[order])
    buf_g = jnp.zeros((P,), F32).at[dest].set(flat_g[order])
    blk_e = jnp.minimum(jnp.searchsorted(pad_end, jnp.arange(nblk) * MOE_BLOCK, side='right'), N_EXPERTS - 1)
    x_pad = jnp.concatenate([xf, jnp.zeros((1, D), xf.dtype)], axis=0)
    xb = x_pad[buf_tok].reshape(nblk, MOE_BLOCK, D)

    def expert_block(args):
        xe, e = args
        h = jax.nn.silu(xe @ wg[e]) * (xe @ wu[e])
        return h @ wd[e]

    yb = lax.map(expert_block, (xb, blk_e)).reshape(P, D)
    y = jnp.zeros((N + 1, D), F32).at[buf_tok].add(yb.astype(F32) * buf_g[:, None])[:N]
    return y.reshape(Bx, T, D).astype(x.dtype)


def trunk(x, pos, W, lbs, past):
    states = []
    for l in range(DEPTH):
        mix, st = token_mixer(rmsnorm(x, W['attn_norm'][l]), pos, W, l, lbs[l], past)
        x = x + mix.astype(x.dtype)
        h = rmsnorm(x, W['ffn_norm'][l])
        j = l // 2
        if l % 2 == 0:
            x = x + swiglu(h, W['dense_wg'][j], W['dense_wu'][j], W['dense_wd'][j]).astype(x.dtype)
        else:
            x = x + moe_ffn(h, W['moe_router'][j], W['moe_router_b'][j], W['moe_wg'][j],
                            W['moe_wu'][j], W['moe_wd'][j])
        states.append(st)
    stacked = {name: jnp.stack([s[name] for s in states]) for name in states[0]}
    return x, stacked


def setup_inputs(seed: int = 0) -> dict:
    key = jax.random.key(seed)
    ks = iter(jax.random.split(key, 40))

    def nrm(shape, scale=1.0):
        return jax.random.normal(next(ks), shape, F32) * scale

    def gain(shape):
        return 1.0 + nrm(shape, 0.05)

    return {
        "x_prompt": nrm((BATCH, SEQ, D_MODEL)),
        "x_sample": nrm((DEC_BATCH, DEC_SEQ, D_MODEL)),
        "cache_k": nrm((DEPTH, DEC_BATCH, PAST_LEN, A_KV_HEADS, HEAD_DIM)),
        "cache_v": nrm((DEPTH, DEC_BATCH, PAST_LEN, A_KV_HEADS, HEAD_DIM)),
        "cache_kidx": nrm((DEPTH, DEC_BATCH, PAST_LEN, IDX_DIM)),
        "state_hgrn": nrm((DEPTH, DEC_BATCH, B_HEADS, B_DK, B_DV), 0.5),
        "state_mlstm_C": nrm((DEPTH, DEC_BATCH, C_HEADS, C_DK, C_DV)),
        "state_mlstm_n": jnp.abs(nrm((DEPTH, DEC_BATCH, C_HEADS, C_DK))),
        "state_mlstm_m": nrm((DEPTH, DEC_BATCH, C_HEADS)),
        "state_mlstm_conv": nrm((DEPTH, DEC_BATCH, CONV_W - 1, CONV_CH)),
        "attn_norm": gain((DEPTH, D_MODEL)),
        "w_in": nrm((DEPTH, D_MODEL, IN_WIDTH), D_MODEL ** -0.5),
        "w_out": nrm((DEPTH, MIX_WIDTH, D_MODEL), MIX_WIDTH ** -0.5),
        "q_norm": gain((DEPTH, HEAD_DIM)),
        "k_norm": gain((DEPTH, HEAD_DIM)),
        "hgrn_lb_logits": nrm((DEPTH, B_HEADS * B_DK), 0.5),
        "hgrn_norm": gain((DEPTH, B_DV)),
        "conv_w": nrm((DEPTH, CONV_W, CONV_CH), CONV_W ** -0.5),
        "conv_b": nrm((DEPTH, CONV_CH), 0.02),
        "ig_b": nrm((DEPTH, C_HEADS), 0.1),
        "fg_b": 3.0 + nrm((DEPTH, C_HEADS), 0.5),
        "mlstm_norm": gain((DEPTH, C_DV)),
        "ffn_norm": gain((DEPTH, D_MODEL)),
        "dense_wg": nrm((N_DENSE, D_MODEL, D_FF), D_MODEL ** -0.5),
        "dense_wu": nrm((N_DENSE, D_MODEL, D_FF), D_MODEL ** -0.5),
        "dense_wd": nrm((N_DENSE, D_FF, D_MODEL), D_FF ** -0.5),
        "moe_router": nrm((N_MOE, D_MODEL, N_EXPERTS), D_MODEL ** -0.5),
        "moe_router_b": nrm((N_MOE, N_EXPERTS), 0.01),
        "moe_wg": nrm((N_MOE, N_EXPERTS, D_MODEL, D_FF_EXPERT), D_MODEL ** -0.5),
        "moe_wu": nrm((N_MOE, N_EXPERTS, D_MODEL, D_FF_EXPERT), D_MODEL ** -0.5),
        "moe_wd": nrm((N_MOE, N_EXPERTS, D_FF_EXPERT, D_MODEL), D_FF_EXPERT ** -0.5),
    }


def reference(x_prompt, x_sample, cache_k, cache_v, cache_kidx, state_hgrn, state_mlstm_C, state_mlstm_n,
              state_mlstm_m, state_mlstm_conv, attn_norm, w_in, w_out, q_norm, k_norm, hgrn_lb_logits,
              hgrn_norm, conv_w, conv_b, ig_b, fg_b, mlstm_norm, ffn_norm, dense_wg, dense_wu, dense_wd,
              moe_router, moe_router_b, moe_wg, moe_wu, moe_wd):
    W = dict(attn_norm=attn_norm, w_in=w_in, w_out=w_out, q_norm=q_norm, k_norm=k_norm,
             hgrn_norm=hgrn_norm, conv_w=conv_w, conv_b=conv_b, ig_b=ig_b, fg_b=fg_b,
             mlstm_norm=mlstm_norm, ffn_norm=ffn_norm, dense_wg=dense_wg, dense_wu=dense_wu,
             dense_wd=dense_wd, moe_router=moe_router, moe_router_b=moe_router_b, moe_wg=moe_wg,
             moe_wu=moe_wu, moe_wd=moe_wd)
    p_lb = jax.nn.softmax(hgrn_lb_logits.astype(F32), axis=0)
    lbs = jnp.cumsum(p_lb, axis=0) - p_lb[0]
    pos_p = jnp.arange(x_prompt.shape[1], dtype=jnp.int32)
    y_prompt, sp = trunk(x_prompt, pos_p, W, lbs, None)
    past = dict(k=cache_k, v=cache_v, ki=cache_kidx, hgrn=state_hgrn, C=state_mlstm_C, n=state_mlstm_n,
                m=state_mlstm_m, conv=state_mlstm_conv)
    pos_s = cache_k.shape[2] + jnp.arange(x_sample.shape[1], dtype=jnp.int32)
    y_sample, ss = trunk(x_sample, pos_s, W, lbs, past)
    return (y_prompt, y_sample,
            sp['k'], sp['v'], sp['ki'], sp['hgrn'], sp['C'], sp['n'], sp['m'], sp['conv'],
            ss['k'], ss['v'], ss['ki'], ss['hgrn'], ss['C'], ss['n'], ss['m'], ss['conv'])
```

```python
import functools
import math

import numpy as np
import jax
import jax.numpy as jnp
from jax import lax
from jax.experimental import pallas as pl
from jax.experimental.pallas import tpu as pltpu

F32 = jnp.float32
BF16 = jnp.bfloat16
I32 = jnp.int32

CHUNK = 64
EPS = 1e-6
NEG = -1e30
HEAD_DIM = 128
A_HEADS = 8
A_KV_HEADS = 2
GQA = A_HEADS // A_KV_HEADS
IDX_HEADS = 8
IDX_DIM = 64
TOPK_MAX = 256
ROPE_THETA = 500000.0
ROT_FRAC = 4
ATTN_SCALE = HEAD_DIM ** -0.5
IDX_SCALE = (IDX_HEADS * IDX_DIM) ** -0.5
B_HEADS = 4
B_DK = 128
B_DV = 128
C_HEADS = 4
C_DK = 64
C_DV = 128
CONV_W = 4
CONV_CH = 2 * C_HEADS * C_DK
N_EXPERTS = 8
TOP_K = 2

COL_AQ, COL_AK, COL_AV, COL_IQ = 0, 1024, 1280, 1536
COL_BQ, COL_BF, COL_BI, COL_BG = 2048, 2560, 3072, 3584
COL_CQK, COL_CV, COL_CO, COL_MISC = 4096, 4608, 5120, 5632
Z_WIDTH = 5760
MISC_IK, MISC_IW, MISC_CI, MISC_CF = 0, 64, 72, 76

LANES = 128
VMEM_LIMIT_BYTES = 56 * 1024 * 1024
INT_MIN = -2 ** 31

HGRN_SUB = 16
KEY_BLOCK = 512
GATHER_ROWS = 256


def _cparams(*sem):
    return pltpu.CompilerParams(dimension_semantics=sem, vmem_limit_bytes=VMEM_LIMIT_BYTES)


def _sigmoid(x):
    return 1.0 / (1.0 + jnp.exp(-x))


def _silu(x):
    return x * _sigmoid(x)


def _row_tile(n, pref):
    t = min(n, pref)
    while n % t:
        t //= 2
    return t


def _norm_matmul_kernel(x_ref, g_ref, w_ref, o_ref, xn_ref):
    @pl.when(pl.program_id(1) == 0)
    def _():
        x = x_ref[...]
        y = x * lax.rsqrt(jnp.mean(x * x, axis=-1, keepdims=True) + EPS) * g_ref[...]
        xn_ref[...] = y.astype(BF16)

    o_ref[...] = jnp.dot(xn_ref[...], w_ref[...], preferred_element_type=F32)


def _norm_matmul(x, g, w, tn):
    n, d = x.shape
    nout = w.shape[1]
    tm = _row_tile(n, 512)
    return pl.pallas_call(
        _norm_matmul_kernel,
        grid=(n // tm, nout // tn),
        in_specs=[pl.BlockSpec((tm, d), lambda i, j: (i, 0)),
                  pl.BlockSpec((1, d), lambda i, j: (0, 0)),
                  pl.BlockSpec((d, tn), lambda i, j: (0, j))],
        out_specs=pl.BlockSpec((tm, tn), lambda i, j: (i, j)),
        out_shape=jax.ShapeDtypeStruct((n, nout), F32),
        scratch_shapes=[pltpu.VMEM((tm, d), BF16)],
        compiler_params=_cparams("parallel", "arbitrary"),
        name="norm_in_proj",
    )(x, g.reshape(1, d), w)


def _rope(x, c, sa, sb, half):
    w = x.shape[-1]
    return x * c + pltpu.roll(x, w - half, 1) * sa + pltpu.roll(x, half, 1) * sb


def _prep_kernel(zq_ref, zk_ref, ziq_ref, zm_ref, qn_ref, kn_ref, t128_ref, t64_ref,
                 q_ref, k_ref, qi_ref, ki_ref):
    c1, sa1, sb1 = t128_ref[0], t128_ref[1], t128_ref[2]
    c2, sa2, sb2 = t64_ref[0], t64_ref[1], t64_ref[2]
    half1 = HEAD_DIM // ROT_FRAC // 2
    half2 = IDX_DIM // ROT_FRAC // 2

    def normed(x, g):
        return x * lax.rsqrt(jnp.mean(x * x, axis=-1, keepdims=True) + EPS) * g

    for h in range(A_HEADS):
        x = zq_ref[0, :, h * HEAD_DIM:(h + 1) * HEAD_DIM]
        q_ref[0, h] = _rope(normed(x, qn_ref[...]), c1, sa1, sb1, half1)
    for h in range(A_KV_HEADS):
        x = zk_ref[0, :, h * HEAD_DIM:(h + 1) * HEAD_DIM]
        k_ref[0, :, h * HEAD_DIM:(h + 1) * HEAD_DIM] = _rope(normed(x, kn_ref[...]), c1, sa1, sb1, half1)
    for j in range(IDX_HEADS * IDX_DIM // LANES):
        x = ziq_ref[0, :, j * LANES:(j + 1) * LANES]
        qi_ref[0, :, j * LANES:(j + 1) * LANES] = _rope(x, c2, sa2, sb2, half2)
    m = _rope(zm_ref[0], c2, sa2, sb2, half2)
    ki_ref[0] = m[:, MISC_IK:MISC_IK + IDX_DIM]


def _rope_tables(pos, period):
    rot = period // ROT_FRAC
    half = rot // 2
    inv = jnp.exp(-math.log(ROPE_THETA) * 2.0 * jnp.arange(half, dtype=F32) / rot)
    ang = pos.astype(F32)[:, None] * inv
    cos, sin = jnp.cos(ang), jnp.sin(ang)
    t = pos.shape[0]
    ones = jnp.ones((t, period - rot), F32)
    zeros_h = jnp.zeros((t, half), F32)
    zeros_r = jnp.zeros((t, period - rot), F32)
    c = jnp.concatenate([cos, cos, ones], axis=1)
    sa = jnp.concatenate([-sin, zeros_h, zeros_r], axis=1)
    sb = jnp.concatenate([zeros_h, sin, zeros_r], axis=1)
    tab = jnp.stack([c, sa, sb])
    return jnp.tile(tab, (1, 1, LANES // period))


def _prep(z3, q_norm, k_norm, pos):
    bsz, t, _ = z3.shape
    tt = _row_tile(t, 256)
    t128 = _rope_tables(pos, HEAD_DIM)
    t64 = _rope_tables(pos, IDX_DIM)

    def col(width, off):
        return pl.BlockSpec((1, tt, width), lambda b, i: (b, i, off // width))

    return pl.pallas_call(
        _prep_kernel,
        grid=(bsz, t // tt),
        in_specs=[col(1024, COL_AQ), col(256, COL_AK), col(512, COL_IQ), col(128, COL_MISC),
                  pl.BlockSpec((1, HEAD_DIM), lambda b, i: (0, 0)),
                  pl.BlockSpec((1, HEAD_DIM), lambda b, i: (0, 0)),
                  pl.BlockSpec((3, tt, LANES), lambda b, i: (0, i, 0)),
                  pl.BlockSpec((3, tt, LANES), lambda b, i: (0, i, 0))],
        out_specs=[pl.BlockSpec((1, A_HEADS, tt, HEAD_DIM), lambda b, i: (b, 0, i, 0)),
                   pl.BlockSpec((1, tt, A_KV_HEADS * HEAD_DIM), lambda b, i: (b, i, 0)),
                   pl.BlockSpec((1, tt, IDX_HEADS * IDX_DIM), lambda b, i: (b, i, 0)),
                   pl.BlockSpec((1, tt, IDX_DIM), lambda b, i: (b, i, 0))],
        out_shape=[jax.ShapeDtypeStruct((bsz, A_HEADS, t, HEAD_DIM), F32),
                   jax.ShapeDtypeStruct((bsz, t, A_KV_HEADS * HEAD_DIM), F32),
                   jax.ShapeDtypeStruct((bsz, t, IDX_HEADS * IDX_DIM), F32),
                   jax.ShapeDtypeStruct((bsz, t, IDX_DIM), F32)],
        compiler_params=_cparams("parallel", "parallel"),
        name="qk_prep",
    )(z3, z3, z3, z3, q_norm.reshape(1, HEAD_DIM), k_norm.reshape(1, HEAD_DIM), t128, t64)


def _attn_kernel(q_ref, qi_ref, zm_ref, k_ref, v_ref, ki_ref, o_ref, key_ref, *,
                 tq, n_keys, topk, q_pos0, kb_size):
    j = pl.program_id(1)
    q_first = q_pos0 + j * tq
    lim = jnp.minimum(((q_first + tq - 1) // CHUNK + 1) * CHUNK, n_keys)
    n_kb = (lim + kb_size - 1) // kb_size

    row = lax.broadcasted_iota(I32, (tq, kb_size), 0)
    lane = lax.broadcasted_iota(I32, (tq, kb_size), 1)
    q_chunk = lax.shift_right_logical(q_first + row, 6)

    qi = qi_ref[0].astype(BF16)
    wi = zm_ref[0][:, MISC_IW:MISC_IW + IDX_HEADS] * IDX_SCALE

    def score_body(kb, carry):
        k0 = pl.multiple_of(kb * kb_size, kb_size)
        ki_blk = ki_ref[0, pl.ds(k0, kb_size), :].astype(BF16)
        s = jnp.zeros((tq, kb_size), F32)
        for h in range(IDX_HEADS):
            r = lax.dot_general(qi[:, h * IDX_DIM:(h + 1) * IDX_DIM], ki_blk,
                                (((1,), (1,)), ((), ())), preferred_element_type=F32)
            s = s + jnp.maximum(r, 0.0) * wi[:, h:h + 1]
        kidx = k0 + lane
        admiss = (lax.shift_right_logical(kidx, 6) <= q_chunk) & (kidx < n_keys)
        s = jnp.where(admiss, s, NEG)
        bits = lax.bitcast_convert_type(s, I32)
        key_ref[kb] = jnp.where(bits < 0, bits ^ 0x7FFFFFFF, bits)
        return carry

    lax.fori_loop(0, n_kb, score_body, 0)

    def count(pred_fn):
        def body(kb, acc):
            m = pred_fn(key_ref[kb], kb).astype(I32)
            for c in range(kb_size // LANES):
                acc = acc + m[:, c * LANES:(c + 1) * LANES]
            return acc
        acc = lax.fori_loop(0, n_kb, body, jnp.zeros((tq, LANES), I32))
        return jnp.sum(acc, axis=1, keepdims=True)

    c0 = count(lambda key, kb: key >= 0)
    prefix = jnp.where(c0 >= topk, jnp.zeros((tq, 1), I32), jnp.full((tq, 1), INT_MIN, I32))

    def bit_body(i, prefix):
        cand = prefix | lax.shift_left(jnp.int32(1), 30 - i)
        c = count(lambda key, kb: key >= cand)
        return jnp.where(c >= topk, cand, prefix)

    thr = lax.fori_loop(0, 31, bit_body, prefix)

    c_gt = count(lambda key, kb: key > thr)
    c_ge = count(lambda key, kb: key >= thr)
    need = topk - c_gt
    excess = jnp.max(c_ge - c_gt - need)

    def tie_limit():
        def idx_body(i, x):
            cand = x | lax.shift_left(jnp.int32(1), 14 - i)
            c = count(lambda key, kb: (key == thr) & (kb * kb_size + lane < cand))
            return jnp.where(c < need, cand, x)
        return lax.fori_loop(0, 15, idx_body, jnp.zeros((tq, 1), I32))

    xlim = lax.cond(excess > 0, tie_limit, lambda: jnp.full((tq, 1), 2 ** 30, I32))

    neg_key = int(np.float32(0.5 * NEG).view(np.int32)) ^ 0x7FFFFFFF

    def mask_body(kb, carry):
        key = key_ref[kb]
        sel = (key > thr) | ((key == thr) & (kb * kb_size + lane <= xlim))
        key_ref[kb] = (sel & (key > neg_key)).astype(I32)
        return carry

    lax.fori_loop(0, n_kb, mask_body, 0)

    for h in range(A_HEADS):
        g = h // GQA
        qh = q_ref[0, h].astype(BF16)

        def att_body(kb, carry, qh=qh, g=g):
            m_i, l_i, acc = carry
            k0 = pl.multiple_of(kb * kb_size, kb_size)
            k_blk = k_ref[0, pl.ds(k0, kb_size), g * HEAD_DIM:(g + 1) * HEAD_DIM].astype(BF16)
            v_blk = v_ref[0, pl.ds(k0, kb_size), g * HEAD_DIM:(g + 1) * HEAD_DIM].astype(BF16)
            s = lax.dot_general(qh, k_blk, (((1,), (1,)), ((), ())), preferred_element_type=F32) * ATTN_SCALE
            s = jnp.where(key_ref[kb] != 0, s, NEG)
            m_new = jnp.maximum(m_i, jnp.max(s, axis=1, keepdims=True))
            alpha = jnp.exp(m_i - m_new)
            p = jnp.exp(s - m_new)
            l_new = alpha * l_i + jnp.sum(p, axis=1, keepdims=True)
            acc_new = alpha * acc + jnp.dot(p.astype(BF16), v_blk, preferred_element_type=F32)
            return m_new, l_new, acc_new

        init = (jnp.full((tq, 1), NEG, F32), jnp.zeros((tq, 1), F32), jnp.zeros((tq, HEAD_DIM), F32))
        _, l_f, acc_f = lax.fori_loop(0, n_kb, att_body, init)
        o_ref[0, :, h * HEAD_DIM:(h + 1) * HEAD_DIM] = acc_f / l_f


def _sparse_attention(q, qi, z3, k_all, v_src, v_col, ki_all, n_keys, q_pos0):
    bsz, _, t, _ = q.shape
    lp = k_all.shape[1]
    tq = min(128, t)
    topk = min(TOPK_MAX, n_keys // 4)
    kb_size = KEY_BLOCK
    assert lp % kb_size == 0 and kb_size >= topk
    kern = functools.partial(_attn_kernel, tq=tq, n_keys=n_keys, topk=topk, q_pos0=q_pos0, kb_size=kb_size)
    return pl.pallas_call(
        kern,
        grid=(bsz, t // tq),
        in_specs=[pl.BlockSpec((1, A_HEADS, tq, HEAD_DIM), lambda b, i: (b, 0, i, 0)),
                  pl.BlockSpec((1, tq, IDX_HEADS * IDX_DIM), lambda b, i: (b, i, 0)),
                  pl.BlockSpec((1, tq, LANES), lambda b, i: (b, i, COL_MISC // LANES)),
                  pl.BlockSpec((1, lp, 256), lambda b, i: (b, 0, 0)),
                  pl.BlockSpec((1, lp, 256), lambda b, i: (b, 0, v_col)),
                  pl.BlockSpec((1, lp, IDX_DIM), lambda b, i: (b, 0, 0))],
        out_specs=pl.BlockSpec((1, tq, A_HEADS * HEAD_DIM), lambda b, i: (b, i, 0)),
        out_shape=jax.ShapeDtypeStruct((bsz, t, A_HEADS * HEAD_DIM), F32),
        scratch_shapes=[pltpu.VMEM((lp // kb_size, tq, kb_size), I32)],
        compiler_params=_cparams("parallel", "arbitrary"),
        name="sparse_attention",
    )(q, qi, z3, k_all, v_src, ki_all)


def _hgrn_kernel(zq_ref, zf_ref, zi_ref, zg_ref, lb_ref, gn_ref, s0_ref, ob_ref, s1_ref, st_ref, *, tb):
    c = pl.program_id(1)
    sub = HGRN_SUB

    @pl.when(c == 0)
    def _():
        for h in range(B_HEADS):
            st_ref[h] = s0_ref[0, h].T

    rr = lax.broadcasted_iota(I32, (sub, sub), 0)
    cc = lax.broadcasted_iota(I32, (sub, sub), 1)
    tril = (rr >= cc).astype(F32)
    rowi = lax.broadcasted_iota(I32, (sub, B_DK), 0)

    def step(i, carry):
        r0 = pl.multiple_of(i * sub, sub)
        zq = zq_ref[0, pl.ds(r0, sub), :]
        zf = zf_ref[0, pl.ds(r0, sub), :]
        zi = zi_ref[0, pl.ds(r0, sub), :]
        zg = zg_ref[0, pl.ds(r0, sub), :]
        for h in range(B_HEADS):
            sl = slice(h * B_DK, (h + 1) * B_DK)
            lb = lb_ref[:, sl]
            f = lb + (1.0 - lb) * _sigmoid(zf[:, sl])
            logf = jnp.log(jnp.maximum(f, 1e-30))
            k = 1.0 - f
            q = _silu(zq[:, sl])
            v = zi[:, sl]
            b = jnp.dot(tril, logf, precision=lax.Precision.HIGHEST, preferred_element_type=F32)
            b_last = b[sub - 1:sub, :]
            st = st_ref[h]
            o = lax.dot_general((q * jnp.exp(b)).astype(BF16), st.astype(BF16),
                                (((1,), (1,)), ((), ())), preferred_element_type=F32)
            for s in range(sub):
                w = jnp.exp(jnp.where(rowi >= s, b - b[s:s + 1, :], NEG))
                a = jnp.sum(q * w * k[s:s + 1, :], axis=-1, keepdims=True)
                o = o + a * v[s:s + 1, :]
            y = o * lax.rsqrt(jnp.mean(o * o, axis=-1, keepdims=True) + EPS) * gn_ref[...]
            ob_ref[0, pl.ds(r0, sub), sl] = y * _silu(zg[:, sl])
            kt = k * jnp.exp(b_last - b)
            upd = lax.dot_general(v.astype(BF16), kt.astype(BF16), (((0,), (0,)), ((), ())),
                                  preferred_element_type=F32)
            st_ref[h] = st * jnp.exp(b_last) + upd
        return carry

    lax.fori_loop(0, tb // sub, step, 0)

    @pl.when(c == pl.num_programs(1) - 1)
    def _():
        for h in range(B_HEADS):
            s1_ref[0, h] = st_ref[h].T


def _hgrn(z3, lb, gn, s0):
    bsz, t, _ = z3.shape
    tb = _row_tile(t, 256)
    width = B_HEADS * B_DK

    def col(off):
        return pl.BlockSpec((1, tb, width), lambda b, i: (b, i, off // width))

    return pl.pallas_call(
        functools.partial(_hgrn_kernel, tb=tb),
        grid=(bsz, t // tb),
        in_specs=[col(COL_BQ), col(COL_BF), col(COL_BI), col(COL_BG),
                  pl.BlockSpec((1, width), lambda b, i: (0, 0)),
                  pl.BlockSpec((1, B_DV), lambda b, i: (0, 0)),
                  pl.BlockSpec((1, B_HEADS, B_DK, B_DV), lambda b, i: (b, 0, 0, 0))],
        out_specs=[pl.BlockSpec((1, tb, width), lambda b, i: (b, i, 0)),
                   pl.BlockSpec((1, B_HEADS, B_DK, B_DV), lambda b, i: (b, 0, 0, 0))],
        out_shape=[jax.ShapeDtypeStruct((bsz, t, width), F32),
                   jax.ShapeDtypeStruct((bsz, B_HEADS, B_DK, B_DV), F32)],
        scratch_shapes=[pltpu.VMEM((B_HEADS, B_DV, B_DK), F32)],
        compiler_params=_cparams("parallel", "arbitrary"),
        name="hgrn2",
    )(z3, z3, z3, z3, lb.reshape(1, width), gn.reshape(1, B_DV), s0)


CONV_PAD = 8


def _mlstm_kernel(zqk_ref, zv_ref, zo_ref, zm_ref, cw_ref, cb_ref, gb_ref, gn_ref,
                  c0_ref, n0_ref, m0_ref, cv0_ref,
                  hc_ref, c1_ref, n1_ref, m1_ref,
                  xe_ref, qk_ref, c_ref, n_ref, m_ref, *, tb):
    ci = pl.program_id(1)
    keep = CONV_W - 1

    @pl.when(ci == 0)
    def _():
        xe_ref[CONV_PAD - keep:CONV_PAD, :] = cv0_ref[0]
        c_ref[...] = c0_ref[0]
        n_ref[...] = n0_ref[0]
        m0_all = m0_ref[0]
        for h in range(C_HEADS):
            m_ref[h] = jnp.broadcast_to(m0_all[:, h:h + 1], (1, LANES))

    xe_ref[CONV_PAD:CONV_PAD + tb, :] = zqk_ref[0]
    y = cb_ref[...]
    for jw in range(CONV_W):
        y = y + cw_ref[jw:jw + 1, :] * xe_ref[CONV_PAD - keep + jw:CONV_PAD - keep + jw + tb, :]
    qk_ref[...] = _silu(y)
    xe_ref[CONV_PAD - keep:CONV_PAD, :] = zqk_ref[0, tb - keep:tb, :]

    n = CHUNK
    rr = lax.broadcasted_iota(I32, (n, n), 0)
    cc = lax.broadcasted_iota(I32, (n, n), 1)
    causal = rr >= cc
    tril = causal.astype(F32)

    def chunk(i, carry):
        r0 = pl.multiple_of(i * n, n)
        gm = zm_ref[0, pl.ds(r0, n), :] + gb_ref[...]
        lf = jnp.minimum(gm, 0.0) - jnp.log(1.0 + jnp.exp(-jnp.abs(gm)))
        bc = jnp.dot(tril, lf, precision=lax.Precision.HIGHEST, preferred_element_type=F32)
        gmt = gm.T
        bct = bc.T
        qk = qk_ref[pl.ds(r0, n), :]
        vv = zv_ref[0, pl.ds(r0, n), :]
        zo = zo_ref[0, pl.ds(r0, n), :]
        for h in range(C_HEADS):
            q = qk[:, h * C_DK:(h + 1) * C_DK]
            k = qk[:, CONV_CH // 2 + h * C_DK:CONV_CH // 2 + (h + 1) * C_DK] * (C_DK ** -0.5)
            v = vv[:, h * C_DV:(h + 1) * C_DV]
            b_col = bc[:, MISC_CF + h:MISC_CF + h + 1]
            b_row = bct[MISC_CF + h:MISC_CF + h + 1, :]
            ig_col = gm[:, MISC_CI + h:MISC_CI + h + 1]
            ig_row = gmt[MISC_CI + h:MISC_CI + h + 1, :]
            m0 = m_ref[h][:, 0:1]
            c0 = c_ref[h]
            n0 = n_ref[h:h + 1, :]
            d = jnp.where(causal, b_col - b_row + ig_row, NEG)
            inter = b_col + m0
            m = jnp.maximum(inter, jnp.max(d, axis=1, keepdims=True))
            w_d = jnp.exp(d - m)
            w_i = jnp.exp(inter - m)
            qb = q.astype(BF16)
            s = lax.dot_general(qb, k.astype(BF16), (((1,), (1,)), ((), ())), preferred_element_type=F32) * w_d
            num = jnp.dot(s.astype(BF16), v.astype(BF16), preferred_element_type=F32) + \
                w_i * jnp.dot(qb, c0.astype(BF16), preferred_element_type=F32)
            den = jnp.sum(s, axis=1, keepdims=True) + w_i * jnp.sum(q * n0, axis=1, keepdims=True)
            hh = num / jnp.maximum(jnp.abs(den), jnp.exp(-m))
            m_t = m[n - 1:n, :]
            b_last = b_col[n - 1:n, :]
            w_s = jnp.exp(b_last - b_col + ig_col - m_t)
            w_c = jnp.exp(b_last + m0 - m_t)
            ks = k * w_s
            c_ref[h] = w_c * c0 + lax.dot_general(ks.astype(BF16), v.astype(BF16), (((0,), (0,)), ((), ())),
                                                  preferred_element_type=F32)
            n_ref[h:h + 1, :] = w_c * n0 + jnp.sum(ks, axis=0, keepdims=True)
            m_ref[h] = jnp.broadcast_to(m_t, (1, LANES))
            yn = hh * lax.rsqrt(jnp.mean(hh * hh, axis=-1, keepdims=True) + EPS) * gn_ref[...]
            hc_ref[0, pl.ds(r0, n), h * C_DV:(h + 1) * C_DV] = yn * _sigmoid(zo[:, h * C_DV:(h + 1) * C_DV])
        return carry

    lax.fori_loop(0, tb // n, chunk, 0)

    @pl.when(ci == pl.num_programs(1) - 1)
    def _():
        c1_ref[0] = c_ref[...]
        n1_ref[0] = n_ref[...]
        m1_ref[0] = jnp.concatenate([m_ref[h][:, 0:1] for h in range(C_HEADS)], axis=1)


def _mlstm(z3, conv_w, conv_b, ig_b, fg_b, gn, c0, n0, m0, cv0):
    bsz, t, _ = z3.shape
    tb = _row_tile(t, 256)
    assert tb % CHUNK == 0
    width = 512
    gate_bias = jnp.zeros((1, LANES), F32)
    gate_bias = gate_bias.at[0, MISC_CI:MISC_CI + C_HEADS].set(ig_b.astype(F32))
    gate_bias = gate_bias.at[0, MISC_CF:MISC_CF + C_HEADS].set(fg_b.astype(F32))

    def col(off, w=width):
        return pl.BlockSpec((1, tb, w), lambda b, i: (b, i, off // w))

    def full(shape):
        nd = len(shape)
        return pl.BlockSpec(shape, lambda b, i: (0,) * nd)

    def per_b(shape):
        nd = len(shape)
        return pl.BlockSpec((1,) + shape, lambda b, i: (b,) + (0,) * nd)

    return pl.pallas_call(
        functools.partial(_mlstm_kernel, tb=tb),
        grid=(bsz, t // tb),
        in_specs=[col(COL_CQK), col(COL_CV), col(COL_CO), col(COL_MISC, LANES),
                  full((CONV_W, CONV_CH)), full((1, CONV_CH)), full((1, LANES)), full((1, C_DV)),
                  per_b((C_HEADS, C_DK, C_DV)), per_b((C_HEADS, C_DK)), per_b((1, C_HEADS)),
                  per_b((CONV_W - 1, CONV_CH))],
        out_specs=[pl.BlockSpec((1, tb, width), lambda b, i: (b, i, 0)),
                   per_b((C_HEADS, C_DK, C_DV)), per_b((C_HEADS, C_DK)), per_b((1, C_HEADS))],
        out_shape=[jax.ShapeDtypeStruct((bsz, t, width), F32),
                   jax.ShapeDtypeStruct((bsz, C_HEADS, C_DK, C_DV), F32),
                   jax.ShapeDtypeStruct((bsz, C_HEADS, C_DK), F32),
                   jax.ShapeDtypeStruct((bsz, 1, C_HEADS), F32)],
        scratch_shapes=[pltpu.VMEM((CONV_PAD + tb, CONV_CH), F32),
                        pltpu.VMEM((tb, CONV_CH), F32),
                        pltpu.VMEM((C_HEADS, C_DK, C_DV), F32),
                        pltpu.VMEM((C_HEADS, C_DK), F32),
                        pltpu.VMEM((C_HEADS, 1, LANES), F32)],
        compiler_params=_cparams("parallel", "arbitrary"),
        name="mlstm",
    )(z3, z3, z3, z3, conv_w, conv_b.reshape(1, CONV_CH), gate_bias, gn.reshape(1, C_DV),
      c0, n0, m0.reshape(bsz, 1, C_HEADS), cv0)


def _out_proj_kernel(oa_ref, ob_ref, hc_ref, x_ref, w_ref, o_ref):
    wa = w_ref[0:1024, :]
    wb = w_ref[1024:1536, :]
    wc = w_ref[1536:2048, :]
    acc = jnp.dot(oa_ref[...].astype(BF16), wa, preferred_element_type=F32)
    acc = acc + jnp.dot(ob_ref[...].astype(BF16), wb, preferred_element_type=F32)
    acc = acc + jnp.dot(hc_ref[...].astype(BF16), wc, preferred_element_type=F32)
    o_ref[...] = x_ref[...] + acc


def _out_proj(oa, ob, hc, x, w):
    n, d = x.shape
    tm = _row_tile(n, 512)
    tn = 1024
    return pl.pallas_call(
        _out_proj_kernel,
        grid=(n // tm, d // tn),
        in_specs=[pl.BlockSpec((tm, oa.shape[1]), lambda i, j: (i, 0)),
                  pl.BlockSpec((tm, ob.shape[1]), lambda i, j: (i, 0)),
                  pl.BlockSpec((tm, hc.shape[1]), lambda i, j: (i, 0)),
                  pl.BlockSpec((tm, tn), lambda i, j: (i, j)),
                  pl.BlockSpec((w.shape[0], tn), lambda i, j: (0, j))],
        out_specs=pl.BlockSpec((tm, tn), lambda i, j: (i, j)),
        out_shape=jax.ShapeDtypeStruct((n, d), F32),
        compiler_params=_cparams("parallel", "arbitrary"),
        name="out_proj",
    )(oa, ob, hc, x, w)


def _ffn_kernel(te_ref, nu_ref, x_ref, g_ref, wg_ref, wu_ref, wd_ref, o_ref, hn_ref, *, residual):
    i = pl.program_id(0)
    f = pl.program_id(1)

    @pl.when(i < nu_ref[0])
    def _():
        @pl.when(f == 0)
        def _():
            x = x_ref[...]
            y = x * lax.rsqrt(jnp.mean(x * x, axis=-1, keepdims=True) + EPS) * g_ref[...]
            hn_ref[...] = y.astype(BF16)
            o_ref[...] = x if residual else jnp.zeros_like(x)

        hn = hn_ref[...]
        a = jnp.dot(hn, wg_ref[0], preferred_element_type=F32)
        b = jnp.dot(hn, wu_ref[0], preferred_element_type=F32)
        hmid = (_silu(a) * b).astype(BF16)
        o_ref[...] += jnp.dot(hmid, wd_ref[0], preferred_element_type=F32)

    @pl.when((i >= nu_ref[0]) & (f == 0))
    def _():
        o_ref[...] = jnp.zeros_like(o_ref)


def _ffn(x, g, wg, wu, wd, tile_expert, n_used, tm, residual):
    p, d = x.shape
    ff = wg.shape[2]
    tf = 512
    assert ff % tf == 0 and p % tm == 0

    def w_in_map(i, f, te, nu):
        live = i < nu[0]
        return (te[i], 0, jnp.where(live, f, 0))

    def w_out_map(i, f, te, nu):
        live = i < nu[0]
        return (te[i], jnp.where(live, f, 0), 0)

    grid_spec = pltpu.PrefetchScalarGridSpec(
        num_scalar_prefetch=2,
        grid=(p // tm, ff // tf),
        in_specs=[pl.BlockSpec((tm, d), lambda i, f, te, nu: (i, 0)),
                  pl.BlockSpec((1, d), lambda i, f, te, nu: (0, 0)),
                  pl.BlockSpec((1, d, tf), w_in_map),
                  pl.BlockSpec((1, d, tf), w_in_map),
                  pl.BlockSpec((1, tf, d), w_out_map)],
        out_specs=pl.BlockSpec((tm, d), lambda i, f, te, nu: (i, 0)),
        scratch_shapes=[pltpu.VMEM((tm, d), BF16)],
    )
    return pl.pallas_call(
        functools.partial(_ffn_kernel, residual=residual),
        grid_spec=grid_spec,
        out_shape=jax.ShapeDtypeStruct((p, d), F32),
        compiler_params=_cparams("parallel", "arbitrary"),
        name="swiglu_ffn",
    )(tile_expert, n_used, x, g.reshape(1, d), wg, wu, wd)


def _router_kernel(x_ref, g_ref, w_ref, b_ref, e_ref, p_ref):
    x = x_ref[...]
    h = x * lax.rsqrt(jnp.mean(x * x, axis=-1, keepdims=True) + EPS) * g_ref[...]
    logits = jnp.dot(h, w_ref[...], precision=lax.Precision.HIGHEST, preferred_element_type=F32) + b_ref[...]
    lane = lax.broadcasted_iota(I32, logits.shape, 1)
    logits = jnp.where(lane < N_EXPERTS, logits, -jnp.inf)
    m1 = jnp.max(logits, axis=1, keepdims=True)
    i1 = jnp.min(jnp.where(logits == m1, lane, LANES), axis=1, keepdims=True)
    rest = jnp.where(lane == i1, -jnp.inf, logits)
    m2 = jnp.max(rest, axis=1, keepdims=True)
    i2 = jnp.min(jnp.where(rest == m2, lane, LANES), axis=1, keepdims=True)
    e2 = jnp.exp(m2 - m1)
    den = 1.0 + e2
    e_ref[...] = jnp.where(lane == 0, i1, jnp.where(lane == 1, i2, 0))
    p_ref[...] = jnp.where(lane == 0, 1.0 / den, jnp.where(lane == 1, e2 / den, 0.0))


def _router(x, g, w_router, b_router):
    n, d = x.shape
    tm = _row_tile(n, 512)
    w = jnp.zeros((d, LANES), F32).at[:, :N_EXPERTS].set(w_router.astype(F32))
    b = jnp.zeros((1, LANES), F32).at[0, :N_EXPERTS].set(b_router.astype(F32))
    return pl.pallas_call(
        _router_kernel,
        grid=(n // tm,),
        in_specs=[pl.BlockSpec((tm, d), lambda i: (i, 0)),
                  pl.BlockSpec((1, d), lambda i: (0, 0)),
                  pl.BlockSpec((d, LANES), lambda i: (0, 0)),
                  pl.BlockSpec((1, LANES), lambda i: (0, 0))],
        out_specs=[pl.BlockSpec((tm, LANES), lambda i: (i, 0)),
                   pl.BlockSpec((tm, LANES), lambda i: (i, 0))],
        out_shape=[jax.ShapeDtypeStruct((n, LANES), I32),
                   jax.ShapeDtypeStruct((n, LANES), F32)],
        compiler_params=_cparams("parallel"),
        name="moe_router",
    )(x, g.reshape(1, d), w, b)


def _gather_kernel(idx_ref, src_ref, o_ref, sem):
    rows = o_ref.shape[0]

    def row_copy(r):
        return pltpu.make_async_copy(src_ref.at[pl.ds(idx_ref[0, 0, r], 1), :], o_ref.at[pl.ds(r, 1), :], sem)

    def start(r, c):
        row_copy(r).start()
        return c

    def wait(r, c):
        row_copy(r).wait()
        return c

    lax.fori_loop(0, rows, start, 0)
    lax.fori_loop(0, rows, wait, 0)


def _gather_rows(src, idx):
    p = idx.shape[0]
    d = src.shape[1]
    rows = _row_tile(p, GATHER_ROWS)
    return pl.pallas_call(
        _gather_kernel,
        grid=(p // rows,),
        in_specs=[pl.BlockSpec((1, 1, rows), lambda i: (i, 0, 0), memory_space=pltpu.SMEM),
                  pl.BlockSpec(memory_space=pl.ANY)],
        out_specs=pl.BlockSpec((rows, d), lambda i: (i, 0)),
        out_shape=jax.ShapeDtypeStruct((p, d), src.dtype),
        scratch_shapes=[pltpu.SemaphoreType.DMA(())],
        compiler_params=_cparams("arbitrary"),
        name="row_gather",
    )(idx.reshape(p // rows, 1, rows), src)


def _combine_kernel(x_ref, r1_ref, r2_ref, p_ref, o_ref):
    p = p_ref[...]
    o_ref[...] = x_ref[...] + (r1_ref[...] * p[:, 0:1] + r2_ref[...] * p[:, 1:2])


def _combine(x, r1, r2, gates):
    n, d = x.shape
    tm = _row_tile(n, 512)
    row = pl.BlockSpec((tm, d), lambda i: (i, 0))
    return pl.pallas_call(
        _combine_kernel,
        grid=(n // tm,),
        in_specs=[row, row, row, pl.BlockSpec((tm, LANES), lambda i: (i, 0))],
        out_specs=row,
        out_shape=jax.ShapeDtypeStruct((n, d), F32),
        compiler_params=_cparams("parallel"),
        name="moe_combine",
    )(x, r1, r2, gates)


def _moe(x, g, w_router, b_router, wg, wu, wd):
    n, d = x.shape
    e_out, gates = _router(x, g, w_router, b_router)
    flat_e = e_out[:, :TOP_K].reshape(n * TOP_K)
    nk = n * TOP_K
    tm = 512 if nk >= 8192 else 128
    onehot = (flat_e[:, None] == jnp.arange(N_EXPERTS, dtype=I32)[None, :]).astype(I32)
    csum = jnp.cumsum(onehot, axis=0)
    counts = csum[-1]
    rank = jnp.sum((csum - 1) * onehot, axis=1)
    padded = (counts + tm - 1) // tm * tm
    pad_end = jnp.cumsum(padded)
    pad_start = pad_end - padded
    dest = (pad_start[flat_e] + rank).astype(I32)
    n_tiles = (nk + N_EXPERTS * (tm - 1) + tm - 1) // tm
    p = n_tiles * tm
    src_tok = jnp.zeros((p,), I32).at[dest].set(jnp.arange(nk, dtype=I32) // TOP_K)
    tile_expert = jnp.minimum(jnp.searchsorted(pad_end, jnp.arange(n_tiles, dtype=I32) * tm, side='right'),
                              N_EXPERTS - 1).astype(I32)
    n_used = (pad_end[-1] // tm).astype(I32).reshape(1)

    xs = _gather_rows(x, src_tok)
    yb = _ffn(xs, g, wg, wu, wd, tile_expert, n_used, tm, residual=False)
    dest2 = dest.reshape(n, TOP_K)
    r1 = _gather_rows(yb, dest2[:, 0])
    r2 = _gather_rows(yb, dest2[:, 1])
    return _combine(x, r1, r2, gates)


def _reorder_w_in(w):
    offs, o = {}, 0
    for name, width in (("a_q", 1024), ("a_k", 256), ("a_v", 256), ("i_q", 512), ("i_w", 8), ("i_k", 64),
                        ("b_q", 512), ("b_f", 512), ("b_i", 512), ("b_g", 512), ("c_qk", 512), ("c_v", 512),
                        ("c_i", 4), ("c_f", 4), ("c_o", 512)):
        offs[name] = (o, width)
        o += width

    def c(name):
        s, width = offs[name]
        return w[:, s:s + width]

    pad = jnp.zeros((w.shape[0], LANES - (IDX_DIM + IDX_HEADS + 2 * C_HEADS)), w.dtype)
    cols = [c("a_q"), c("a_k"), c("a_v"), c("i_q"), c("b_q"), c("b_f"), c("b_i"), c("b_g"),
            c("c_qk"), c("c_v"), c("c_o"), c("i_k"), c("i_w"), c("c_i"), c("c_f"), pad]
    return jnp.concatenate(cols, axis=1).astype(BF16)


def _trunk(x, pos0, W, lbs, past):
    bsz, t, d = x.shape
    n = bsz * t
    pos = pos0 + jnp.arange(t, dtype=I32)
    xf = x.reshape(n, d)
    states = []
    depth = W['w_in'].shape[0]
    for l in range(depth):
        z = _norm_matmul(xf, W['attn_norm'][l], W['w_in_r'][l], tn=1152)
        z3 = z.reshape(bsz, t, Z_WIDTH)
        q, k, qi, ki = _prep(z3, W['q_norm'][l], W['k_norm'][l], pos)
        v = z3[:, :, COL_AV:COL_AV + 256]
        if past is None:
            n_keys = t
            assert t % KEY_BLOCK == 0
            o_a = _sparse_attention(q, qi, z3, k, z3, COL_AV // 256, ki, n_keys, 0)
            s0 = jnp.zeros((bsz, B_HEADS, B_DK, B_DV), F32)
            c0 = jnp.zeros((bsz, C_HEADS, C_DK, C_DV), F32)
            n0 = jnp.zeros((bsz, C_HEADS, C_DK), F32)
            m0 = jnp.zeros((bsz, C_HEADS), F32)
            cv0 = jnp.zeros((bsz, CONV_W - 1, CONV_CH), F32)
        else:
            pk = past['k'][l].reshape(bsz, -1, 256)
            pv = past['v'][l].reshape(bsz, -1, 256)
            n_keys = pk.shape[1] + t
            lp = (n_keys + KEY_BLOCK - 1) // KEY_BLOCK * KEY_BLOCK
            padk = jnp.zeros((bsz, lp - n_keys, 256), F32)
            k_all = jnp.concatenate([pk, k, padk], axis=1)
            v_all = jnp.concatenate([pv, v, padk], axis=1)
            ki_all = jnp.concatenate([past['ki'][l], ki, padk[:, :, :IDX_DIM]], axis=1)
            o_a = _sparse_attention(q, qi, z3, k_all, v_all, 0, ki_all, n_keys, pk.shape[1])
            s0, c0, n0, m0, cv0 = past['hgrn'][l], past['C'][l], past['n'][l], past['m'][l], past['conv'][l]
        ob, s1 = _hgrn(z3, lbs[l], W['hgrn_norm'][l], s0)
        hc, c1, n1, m1 = _mlstm(z3, W['conv_w'][l], W['conv_b'][l], W['ig_b'][l], W['fg_b'][l],
                                W['mlstm_norm'][l], c0, n0, m0, cv0)
        cqk = z3[:, :, COL_CQK:COL_CQK + CONV_CH]
        conv1 = jnp.concatenate([cv0, cqk], axis=1)[:, -(CONV_W - 1):]
        xf = _out_proj(o_a.reshape(n, -1), ob.reshape(n, -1), hc.reshape(n, -1), xf, W['w_out_b'][l])
        jj = l // 2
        if l % 2 == 0:
            tm = _row_tile(n, 512)
            te = jnp.zeros((n // tm,), I32)
            nu = jnp.full((1,), n // tm, I32)
            xf = _ffn(xf, W['ffn_norm'][l], W['dense_wg_b'][jj][None], W['dense_wu_b'][jj][None],
                      W['dense_wd_b'][jj][None], te, nu, tm, residual=True)
        else:
            xf = _moe(xf, W['ffn_norm'][l], W['moe_router'][jj], W['moe_router_b'][jj],
                      W['moe_wg_b'][jj], W['moe_wu_b'][jj], W['moe_wd_b'][jj])
        states.append(dict(k=k.reshape(bsz, t, A_KV_HEADS, HEAD_DIM), v=v.reshape(bsz, t, A_KV_HEADS, HEAD_DIM),
                           ki=ki, hgrn=s1, C=c1, n=n1, m=m1.reshape(bsz, C_HEADS), conv=conv1))
    stacked = {name: jnp.stack([s[name] for s in states]) for name in states[0]}
    return xf.reshape(bsz, t, d), stacked


def kernel(x_prompt, x_sample, cache_k, cache_v, cache_kidx, state_hgrn, state_mlstm_C, state_mlstm_n,
           state_mlstm_m, state_mlstm_conv, attn_norm, w_in, w_out, q_norm, k_norm, hgrn_lb_logits,
           hgrn_norm, conv_w, conv_b, ig_b, fg_b, mlstm_norm, ffn_norm, dense_wg, dense_wu, dense_wd,
           moe_router, moe_router_b, moe_wg, moe_wu, moe_wd):
    depth = w_in.shape[0]
    W = dict(attn_norm=attn_norm, w_in=w_in, q_norm=q_norm, k_norm=k_norm, hgrn_norm=hgrn_norm,
             conv_w=conv_w, conv_b=conv_b, ig_b=ig_b, fg_b=fg_b, mlstm_norm=mlstm_norm, ffn_norm=ffn_norm,
             moe_router=moe_router, moe_router_b=moe_router_b,
             w_in_r=jnp.stack([_reorder_w_in(w_in[l]) for l in range(depth)]),
             w_out_b=w_out.astype(BF16),
             dense_wg_b=dense_wg.astype(BF16), dense_wu_b=dense_wu.astype(BF16), dense_wd_b=dense_wd.astype(BF16),
             moe_wg_b=moe_wg.astype(BF16), moe_wu_b=moe_wu.astype(BF16), moe_wd_b=moe_wd.astype(BF16))
    p_lb = jax.nn.softmax(hgrn_lb_logits.astype(F32), axis=0)
    lbs = jnp.cumsum(p_lb, axis=0) - p_lb[0]
    y_prompt, sp = _trunk(x_prompt, 0, W, lbs, None)
    past = dict(k=cache_k, v=cache_v, ki=cache_kidx, hgrn=state_hgrn, C=state_mlstm_C, n=state_mlstm_n,
                m=state_mlstm_m, conv=state_mlstm_conv)
    y_sample, ss = _trunk(x_sample, cache_k.shape[2], W, lbs, past)
    return (y_prompt, y_sample,
            sp['k'], sp['v'], sp['ki'], sp['hgrn'], sp['C'], sp['n'], sp['m'], sp['conv'],
            ss['k'], ss['v'], ss['ki'], ss['hgrn'], ss['C'], ss['n'], ss['m'], ss['conv'])
```

```python
import functools
import math

import numpy as np
import jax
import jax.numpy as jnp
from jax import lax
from jax.experimental import pallas as pl
from jax.experimental.pallas import tpu as pltpu

F32 = jnp.float32
BF16 = jnp.bfloat16
I32 = jnp.int32

CHUNK = 64
EPS = 1e-6
NEG = -1e30
HEAD_DIM = 128
A_HEADS = 8
A_KV_HEADS = 2
GQA = A_HEADS // A_KV_HEADS
IDX_HEADS = 8
IDX_DIM = 64
TOPK_MAX = 256
ROPE_THETA = 500000.0
ROT_FRAC = 4
ATTN_SCALE = HEAD_DIM ** -0.5
IDX_SCALE = (IDX_HEADS * IDX_DIM) ** -0.5
B_HEADS = 4
B_DK = 128
B_DV = 128
C_HEADS = 4
C_DK = 64
C_DV = 128
CONV_W = 4
CONV_CH = 2 * C_HEADS * C_DK
N_EXPERTS = 8
TOP_K = 2

COL_AQ, COL_AK, COL_AV, COL_IQ = 0, 1024, 1280, 1536
COL_BQ, COL_BF, COL_BI, COL_BG = 2048, 2560, 3072, 3584
COL_CQK, COL_CV, COL_CO, COL_MISC = 4096, 4608, 5120, 5632
Z_WIDTH = 5760
MISC_IK, MISC_IW, MISC_CI, MISC_CF = 0, 64, 72, 76

LANES = 128
VMEM_LIMIT_BYTES = 56 * 1024 * 1024
INT_MIN = -2 ** 31

HGRN_SUB = 8
KEY_BLOCK = 512
ATTN_ROWS = 256
GATHER_ROWS = 256
MLSTM_BATCH = 2


def _cparams(*sem):
    return pltpu.CompilerParams(dimension_semantics=sem, vmem_limit_bytes=VMEM_LIMIT_BYTES)


def _sigmoid(x):
    return 1.0 / (1.0 + jnp.exp(-x))


def _silu(x):
    return x * _sigmoid(x)


def _row_tile(n, pref):
    t = min(n, pref)
    while n % t:
        t //= 2
    return t


def _norm_matmul_kernel(x_ref, g_ref, w_ref, o_ref, xn_ref):
    @pl.when(pl.program_id(1) == 0)
    def _():
        x = x_ref[...]
        y = x * lax.rsqrt(jnp.mean(x * x, axis=-1, keepdims=True) + EPS) * g_ref[...]
        xn_ref[...] = y.astype(BF16)

    o_ref[...] = jnp.dot(xn_ref[...], w_ref[...], preferred_element_type=F32)


def _norm_matmul(x, g, w, tn):
    n, d = x.shape
    nout = w.shape[1]
    tm = _row_tile(n, 512)
    return pl.pallas_call(
        _norm_matmul_kernel,
        grid=(n // tm, nout // tn),
        in_specs=[pl.BlockSpec((tm, d), lambda i, j: (i, 0)),
                  pl.BlockSpec((1, d), lambda i, j: (0, 0)),
                  pl.BlockSpec((d, tn), lambda i, j: (0, j))],
        out_specs=pl.BlockSpec((tm, tn), lambda i, j: (i, j)),
        out_shape=jax.ShapeDtypeStruct((n, nout), F32),
        scratch_shapes=[pltpu.VMEM((tm, d), BF16)],
        compiler_params=_cparams("parallel", "arbitrary"),
        name="norm_in_proj",
    )(x, g.reshape(1, d), w)


def _rope(x, c, sa, sb, half):
    w = x.shape[-1]
    return x * c + pltpu.roll(x, w - half, 1) * sa + pltpu.roll(x, half, 1) * sb


def _prep_kernel(zq_ref, zk_ref, ziq_ref, zm_ref, qn_ref, kn_ref, t128_ref, t64_ref,
                 q_ref, k_ref, qi_ref, ki_ref):
    c1, sa1, sb1 = t128_ref[0], t128_ref[1], t128_ref[2]
    c2, sa2, sb2 = t64_ref[0], t64_ref[1], t64_ref[2]
    half1 = HEAD_DIM // ROT_FRAC // 2
    half2 = IDX_DIM // ROT_FRAC // 2

    def normed(x, g):
        return x * lax.rsqrt(jnp.mean(x * x, axis=-1, keepdims=True) + EPS) * g

    for h in range(A_HEADS):
        x = zq_ref[0, :, h * HEAD_DIM:(h + 1) * HEAD_DIM]
        q_ref[0, h] = _rope(normed(x, qn_ref[...]), c1, sa1, sb1, half1)
    for h in range(A_KV_HEADS):
        x = zk_ref[0, :, h * HEAD_DIM:(h + 1) * HEAD_DIM]
        k_ref[0, :, h * HEAD_DIM:(h + 1) * HEAD_DIM] = _rope(normed(x, kn_ref[...]), c1, sa1, sb1, half1)
    for j in range(IDX_HEADS * IDX_DIM // LANES):
        x = ziq_ref[0, :, j * LANES:(j + 1) * LANES]
        qi_ref[0, :, j * LANES:(j + 1) * LANES] = _rope(x, c2, sa2, sb2, half2)
    m = _rope(zm_ref[0], c2, sa2, sb2, half2)
    ki_ref[0] = m[:, MISC_IK:MISC_IK + IDX_DIM]


def _rope_tables(pos, period):
    rot = period // ROT_FRAC
    half = rot // 2
    inv = jnp.exp(-math.log(ROPE_THETA) * 2.0 * jnp.arange(half, dtype=F32) / rot)
    ang = pos.astype(F32)[:, None] * inv
    cos, sin = jnp.cos(ang), jnp.sin(ang)
    t = pos.shape[0]
    ones = jnp.ones((t, period - rot), F32)
    zeros_h = jnp.zeros((t, half), F32)
    zeros_r = jnp.zeros((t, period - rot), F32)
    c = jnp.concatenate([cos, cos, ones], axis=1)
    sa = jnp.concatenate([-sin, zeros_h, zeros_r], axis=1)
    sb = jnp.concatenate([zeros_h, sin, zeros_r], axis=1)
    tab = jnp.stack([c, sa, sb])
    return jnp.tile(tab, (1, 1, LANES // period))


def _prep(z3, q_norm, k_norm, pos):
    bsz, t, _ = z3.shape
    tt = _row_tile(t, 256)
    t128 = _rope_tables(pos, HEAD_DIM)
    t64 = _rope_tables(pos, IDX_DIM)

    def col(width, off):
        return pl.BlockSpec((1, tt, width), lambda b, i: (b, i, off // width))

    return pl.pallas_call(
        _prep_kernel,
        grid=(bsz, t // tt),
        in_specs=[col(1024, COL_AQ), col(256, COL_AK), col(512, COL_IQ), col(128, COL_MISC),
                  pl.BlockSpec((1, HEAD_DIM), lambda b, i: (0, 0)),
                  pl.BlockSpec((1, HEAD_DIM), lambda b, i: (0, 0)),
                  pl.BlockSpec((3, tt, LANES), lambda b, i: (0, i, 0)),
                  pl.BlockSpec((3, tt, LANES), lambda b, i: (0, i, 0))],
        out_specs=[pl.BlockSpec((1, A_HEADS, tt, HEAD_DIM), lambda b, i: (b, 0, i, 0)),
                   pl.BlockSpec((1, tt, A_KV_HEADS * HEAD_DIM), lambda b, i: (b, i, 0)),
                   pl.BlockSpec((1, tt, IDX_HEADS * IDX_DIM), lambda b, i: (b, i, 0)),
                   pl.BlockSpec((1, tt, IDX_DIM), lambda b, i: (b, i, 0))],
        out_shape=[jax.ShapeDtypeStruct((bsz, A_HEADS, t, HEAD_DIM), F32),
                   jax.ShapeDtypeStruct((bsz, t, A_KV_HEADS * HEAD_DIM), F32),
                   jax.ShapeDtypeStruct((bsz, t, IDX_HEADS * IDX_DIM), F32),
                   jax.ShapeDtypeStruct((bsz, t, IDX_DIM), F32)],
        compiler_params=_cparams("parallel", "parallel"),
        name="qk_prep",
    )(z3, z3, z3, z3, q_norm.reshape(1, HEAD_DIM), k_norm.reshape(1, HEAD_DIM), t128, t64)


def _f2key(x):
    bits = lax.bitcast_convert_type(x, I32)
    return jnp.where(bits < 0, bits ^ 0x7FFFFFFF, bits)


def _attn_kernel(q_ref, qi_ref, zm_ref, k_ref, v_ref, ki_ref, o_ref,
                 key_ref, bias_ref, qs_ref, kb_ref, vb_ref, kib_ref, acc_ref, m_ref, l_ref, *,
                 tq, n_keys, topk, q_pos0, kb_size, lp):
    j = pl.program_id(1)
    q_first = q_pos0 + j * tq
    lim = jnp.minimum(((q_first + tq - 1) // CHUNK + 1) * CHUNK, n_keys)
    n_kb = (lim + kb_size - 1) // kb_size

    @pl.when(j == 0)
    def _():
        def cast_body(kb, c):
            sl = pl.ds(pl.multiple_of(kb * kb_size, kb_size), kb_size)
            kb_ref[sl, :] = k_ref[0, sl, :].astype(BF16)
            vb_ref[sl, :] = v_ref[0, sl, :].astype(BF16)
            kib_ref[sl, :] = ki_ref[0, sl, :].astype(BF16)
            return c
        lax.fori_loop(0, lp // kb_size, cast_body, 0)

    for h in range(A_HEADS):
        qs_ref[h // GQA, (h % GQA) * tq:(h % GQA + 1) * tq, :] = \
            (q_ref[0, h] * (ATTN_SCALE * math.log2(math.e))).astype(BF16)

    row = lax.broadcasted_iota(I32, (tq, kb_size), 0)
    lane = lax.broadcasted_iota(I32, (tq, kb_size), 1)
    q_chunk = lax.shift_right_logical(q_first + row, 6)

    qi = qi_ref[0].astype(BF16)
    wi = zm_ref[0][:, MISC_IW:MISC_IW + IDX_HEADS] * IDX_SCALE

    def score_body(kb, carry):
        k0 = pl.multiple_of(kb * kb_size, kb_size)
        ki_blk = kib_ref[pl.ds(k0, kb_size), :]
        s = jnp.zeros((tq, kb_size), F32)
        for h in range(IDX_HEADS):
            r = lax.dot_general(qi[:, h * IDX_DIM:(h + 1) * IDX_DIM], ki_blk,
                                (((1,), (1,)), ((), ())), preferred_element_type=F32)
            s = s + jnp.maximum(r, 0.0) * wi[:, h:h + 1]
        kidx = k0 + lane
        admiss = (lax.shift_right_logical(kidx, 6) <= q_chunk) & (kidx < n_keys)
        key_ref[kb] = _f2key(jnp.where(admiss, s, NEG))
        return carry

    lax.fori_loop(0, n_kb, score_body, 0)

    def count(pred_fn):
        def body(kb, acc):
            m = pred_fn(key_ref[kb], kb).astype(I32)
            for c in range(kb_size // LANES):
                acc = acc + m[:, c * LANES:(c + 1) * LANES]
            return acc
        acc = lax.fori_loop(0, n_kb, body, jnp.zeros((tq, LANES), I32))
        return jnp.sum(acc, axis=1, keepdims=True)

    q_chunk1 = lax.shift_right_logical(q_first + lax.broadcasted_iota(I32, (tq, 1), 0), 6)
    n_adm = jnp.minimum((q_chunk1 + 1) * CHUNK, n_keys)
    neg_key = int(np.float32(NEG).view(np.int32)) ^ 0x7FFFFFFF
    few = n_adm <= topk

    def search_cond(st):
        step, prefix, thr, c_thr, done = st
        return (step < 32) & (jnp.min(done) == 0)

    def search_body(st):
        step, prefix, thr, c_thr, done = st
        cand = prefix ^ lax.shift_left(jnp.int32(1), 31 - step)
        c = count(lambda key, kb: key >= cand)
        ge = c >= topk
        hit = (c == topk) & (done == 0)
        return (step + 1, jnp.where(ge, cand, prefix), jnp.where(hit, cand, thr),
                jnp.where(hit, c, c_thr), jnp.where(hit, 1, done))

    init = (jnp.int32(0), jnp.full((tq, 1), INT_MIN, I32), jnp.full((tq, 1), neg_key, I32),
            jnp.full((tq, 1), topk, I32), few.astype(I32))
    step_f, prefix_f, thr, c_ge, done = lax.while_loop(search_cond, search_body, init)
    thr = jnp.where(done != 0, thr, prefix_f)
    c_ge = jnp.where(done != 0, c_ge, topk + 1)

    def tie_limit():
        need = topk - count(lambda key, kb: key > thr)

        def idx_body(i, x):
            cand = x | lax.shift_left(jnp.int32(1), 14 - i)
            c = count(lambda key, kb: (key == thr) & (kb * kb_size + lane < cand))
            return jnp.where(c < need, cand, x)
        return lax.fori_loop(0, 15, idx_body, jnp.zeros((tq, 1), I32))

    xlim = lax.cond(jnp.max(c_ge) > topk, tie_limit, lambda: jnp.full((tq, 1), 2 ** 30, I32))

    valid_key = int(np.float32(0.5 * NEG).view(np.int32)) ^ 0x7FFFFFFF

    def mask_body(kb, carry):
        key = key_ref[kb]
        sel = (key > thr) | ((key == thr) & (kb * kb_size + lane <= xlim))
        bias_ref[kb] = jnp.where(sel & (key > valid_key), 0.0, NEG)
        return carry

    lax.fori_loop(0, n_kb, mask_body, 0)

    n_col = kb_size // LANES

    def logits(g, kb):
        sl = pl.ds(pl.multiple_of(kb * kb_size, kb_size), kb_size)
        k_blk = kb_ref[sl, g * HEAD_DIM:(g + 1) * HEAD_DIM]
        s = lax.dot_general(qs_ref[g], k_blk, (((1,), (1,)), ((), ())), preferred_element_type=F32)
        return s.reshape(GQA, tq, kb_size) + bias_ref[kb][None]

    for g in range(A_KV_HEADS):
        def max_body(kb, mx, g=g):
            s = logits(g, kb)
            for c in range(n_col):
                mx = jnp.maximum(mx, s[:, :, c * LANES:(c + 1) * LANES])
            return mx

        mx = lax.fori_loop(0, n_kb, max_body, jnp.full((GQA, tq, LANES), NEG, F32))
        m_ref[...] = jnp.broadcast_to(jnp.max(mx, axis=2, keepdims=True), (GQA, tq, LANES))
        l_ref[...] = jnp.zeros(l_ref.shape, F32)
        acc_ref[...] = jnp.zeros(acc_ref.shape, F32)

        def att_body(kb, carry, g=g):
            sl = pl.ds(pl.multiple_of(kb * kb_size, kb_size), kb_size)
            v_blk = vb_ref[sl, g * HEAD_DIM:(g + 1) * HEAD_DIM]
            mb = m_ref[...]
            p = jnp.exp2(logits(g, kb) - jnp.concatenate([mb] * n_col, axis=2))
            lsum = l_ref[...]
            for c in range(n_col):
                lsum = lsum + p[:, :, c * LANES:(c + 1) * LANES]
            l_ref[...] = lsum
            pv = jnp.dot(p.reshape(GQA * tq, kb_size).astype(BF16), v_blk, preferred_element_type=F32)
            acc_ref[...] += pv.reshape(GQA, tq, HEAD_DIM)
            return carry

        lax.fori_loop(0, n_kb, att_body, 0)
        out = acc_ref[...] / jnp.sum(l_ref[...], axis=2, keepdims=True)
        for hh in range(GQA):
            h = g * GQA + hh
            o_ref[0, :, h * HEAD_DIM:(h + 1) * HEAD_DIM] = out[hh]


def _sparse_attention(q, qi, z3, k_all, v_src, v_col, ki_all, n_keys, q_pos0):
    bsz, _, t, _ = q.shape
    lp = k_all.shape[1]
    tq = _row_tile(t, ATTN_ROWS)
    topk = min(TOPK_MAX, n_keys // 4)
    kb_size = KEY_BLOCK
    assert lp % kb_size == 0 and kb_size >= topk
    kern = functools.partial(_attn_kernel, tq=tq, n_keys=n_keys, topk=topk, q_pos0=q_pos0, kb_size=kb_size,
                             lp=lp)
    return pl.pallas_call(
        kern,
        grid=(bsz, t // tq),
        in_specs=[pl.BlockSpec((1, A_HEADS, tq, HEAD_DIM), lambda b, i: (b, 0, i, 0)),
                  pl.BlockSpec((1, tq, IDX_HEADS * IDX_DIM), lambda b, i: (b, i, 0)),
                  pl.BlockSpec((1, tq, LANES), lambda b, i: (b, i, COL_MISC // LANES)),
                  pl.BlockSpec((1, lp, 256), lambda b, i: (b, 0, 0)),
                  pl.BlockSpec((1, lp, 256), lambda b, i: (b, 0, v_col)),
                  pl.BlockSpec((1, lp, IDX_DIM), lambda b, i: (b, 0, 0))],
        out_specs=pl.BlockSpec((1, tq, A_HEADS * HEAD_DIM), lambda b, i: (b, i, 0)),
        out_shape=jax.ShapeDtypeStruct((bsz, t, A_HEADS * HEAD_DIM), F32),
        scratch_shapes=[pltpu.VMEM((lp // kb_size, tq, kb_size), I32),
                        pltpu.VMEM((lp // kb_size, tq, kb_size), F32),
                        pltpu.VMEM((A_KV_HEADS, GQA * tq, HEAD_DIM), BF16),
                        pltpu.VMEM((lp, A_KV_HEADS * HEAD_DIM), BF16),
                        pltpu.VMEM((lp, A_KV_HEADS * HEAD_DIM), BF16),
                        pltpu.VMEM((lp, IDX_DIM), BF16),
                        pltpu.VMEM((GQA, tq, HEAD_DIM), F32),
                        pltpu.VMEM((GQA, tq, LANES), F32),
                        pltpu.VMEM((GQA, tq, LANES), F32)],
        compiler_params=_cparams("parallel", "arbitrary"),
        name="sparse_attention",
    )(q, qi, z3, k_all, v_src, ki_all)


def _hgrn_kernel(zq_ref, zf_ref, zi_ref, zg_ref, lb_ref, gn_ref, s0_ref, ob_ref, s1_ref, st_ref, *, tb):
    c = pl.program_id(1)
    sub = HGRN_SUB

    @pl.when(c == 0)
    def _():
        for h in range(B_HEADS):
            st_ref[h] = s0_ref[0, h].T

    n = CHUNK
    ns = n // sub
    rr = lax.broadcasted_iota(I32, (n, n), 0)
    cc = lax.broadcasted_iota(I32, (n, n), 1)
    tril = (rr >= cc).astype(F32)
    rowi = lax.broadcasted_iota(I32, (n, B_DK), 0)
    t_in = lax.broadcasted_iota(I32, (ns, sub, B_DK), 1)
    nt = (((1,), (1,)), ((), ()))

    def step(i, carry):
        r0 = pl.multiple_of(i * n, n)
        zq = zq_ref[0, pl.ds(r0, n), :]
        zf = zf_ref[0, pl.ds(r0, n), :]
        zi = zi_ref[0, pl.ds(r0, n), :]
        zg = zg_ref[0, pl.ds(r0, n), :]
        for h in range(B_HEADS):
            sl = slice(h * B_DK, (h + 1) * B_DK)
            lb = lb_ref[:, sl]
            f = lb + (1.0 - lb) * _sigmoid(zf[:, sl])
            logf = jnp.log(jnp.maximum(f, 1e-30))
            k = 1.0 - f
            q = _silu(zq[:, sl])
            v = zi[:, sl]
            vb = v.astype(BF16)
            b = jnp.dot(tril, logf, precision=lax.Precision.HIGHEST, preferred_element_type=F32)
            b_last = b[n - 1:n, :]
            st = st_ref[h]
            o = lax.dot_general((q * jnp.exp(b)).astype(BF16), st.astype(BF16), nt,
                                preferred_element_type=F32)
            strips = [jnp.zeros((sub, n), F32)]
            for j in range(1, ns):
                ref = b[j * sub - 1:j * sub, :]
                qt = q[j * sub:(j + 1) * sub] * jnp.exp(b[j * sub:(j + 1) * sub] - ref)
                kt = k * jnp.exp(jnp.where(rowi < j * sub, ref - b, NEG))
                strips.append(lax.dot_general(qt.astype(BF16), kt.astype(BF16), nt,
                                              preferred_element_type=F32))
            o = o + jnp.dot(jnp.concatenate(strips, axis=0).astype(BF16), vb, preferred_element_type=F32)
            b3 = b.reshape(ns, sub, B_DK)
            q3 = q.reshape(ns, sub, B_DK)
            k3 = k.reshape(ns, sub, B_DK)
            v3 = v.reshape(ns, sub, B_DV)
            od = jnp.zeros((ns, sub, B_DV), F32)
            for s in range(sub):
                w = jnp.exp(jnp.where(t_in >= s, b3 - b3[:, s:s + 1, :], NEG))
                a = jnp.sum(q3 * w * k3[:, s:s + 1, :], axis=-1, keepdims=True)
                od = od + a * v3[:, s:s + 1, :]
            o = o + od.reshape(n, B_DV)
            y = o * lax.rsqrt(jnp.mean(o * o, axis=-1, keepdims=True) + EPS) * gn_ref[...]
            ob_ref[0, pl.ds(r0, n), sl] = y * _silu(zg[:, sl])
            kt = k * jnp.exp(b_last - b)
            upd = lax.dot_general(vb, kt.astype(BF16), (((0,), (0,)), ((), ())),
                                  preferred_element_type=F32)
            st_ref[h] = st * jnp.exp(b_last) + upd
        return carry

    lax.fori_loop(0, tb // n, step, 0)

    @pl.when(c == pl.num_programs(1) - 1)
    def _():
        for h in range(B_HEADS):
            s1_ref[0, h] = st_ref[h].T


def _hgrn(z3, lb, gn, s0):
    bsz, t, _ = z3.shape
    tb = _row_tile(t, 256)
    width = B_HEADS * B_DK

    def col(off):
        return pl.BlockSpec((1, tb, width), lambda b, i: (b, i, off // width))

    return pl.pallas_call(
        functools.partial(_hgrn_kernel, tb=tb),
        grid=(bsz, t // tb),
        in_specs=[col(COL_BQ), col(COL_BF), col(COL_BI), col(COL_BG),
                  pl.BlockSpec((1, width), lambda b, i: (0, 0)),
                  pl.BlockSpec((1, B_DV), lambda b, i: (0, 0)),
                  pl.BlockSpec((1, B_HEADS, B_DK, B_DV), lambda b, i: (b, 0, 0, 0))],
        out_specs=[pl.BlockSpec((1, tb, width), lambda b, i: (b, i, 0)),
                   pl.BlockSpec((1, B_HEADS, B_DK, B_DV), lambda b, i: (b, 0, 0, 0))],
        out_shape=[jax.ShapeDtypeStruct((bsz, t, width), F32),
                   jax.ShapeDtypeStruct((bsz, B_HEADS, B_DK, B_DV), F32)],
        scratch_shapes=[pltpu.VMEM((B_HEADS, B_DV, B_DK), F32)],
        compiler_params=_cparams("parallel", "arbitrary"),
        name="hgrn2",
    )(z3, z3, z3, z3, lb.reshape(1, width), gn.reshape(1, B_DV), s0)


CONV_PAD = 8


def _mlstm_kernel(zqk_ref, zv_ref, zo_ref, zm_ref, cw_ref, cb_ref, gb_ref, gn_ref,
                  c0_ref, n0_ref, m0_ref, cv0_ref,
                  hc_ref, c1_ref, n1_ref, m1_ref,
                  xe_ref, qk_ref, c_ref, n_ref, m_ref, *, tb, nb):
    ci = pl.program_id(1)
    keep = CONV_W - 1

    @pl.when(ci == 0)
    def _():
        for bb in range(nb):
            xe_ref[bb, CONV_PAD - keep:CONV_PAD, :] = cv0_ref[bb]
            c_ref[bb] = c0_ref[bb]
            n_ref[bb] = n0_ref[bb]
            m0_all = m0_ref[bb]
            for h in range(C_HEADS):
                m_ref[bb, h] = jnp.broadcast_to(m0_all[:, h:h + 1], (1, LANES))

    for bb in range(nb):
        xe_ref[bb, CONV_PAD:CONV_PAD + tb, :] = zqk_ref[bb]
        y = cb_ref[...]
        for jw in range(CONV_W):
            y = y + cw_ref[jw:jw + 1, :] * xe_ref[bb, CONV_PAD - keep + jw:CONV_PAD - keep + jw + tb, :]
        qk_ref[bb] = _silu(y)
        xe_ref[bb, CONV_PAD - keep:CONV_PAD, :] = zqk_ref[bb, tb - keep:tb, :]

    n = CHUNK
    rr = lax.broadcasted_iota(I32, (n, n), 0)
    cc = lax.broadcasted_iota(I32, (n, n), 1)
    causal = rr >= cc
    tril = causal.astype(F32)
    eye_k = (rr == cc).astype(BF16)
    gate_sel = (lax.broadcasted_iota(I32, (2 * C_HEADS, LANES), 1) ==
                lax.broadcasted_iota(I32, (2 * C_HEADS, LANES), 0) + MISC_CI).astype(F32)

    items = [(bb, h) for bb in range(nb) for h in range(C_HEADS)]
    nt = (((1,), (1,)), ((), ()))

    def chunk(i, carry):
        r0 = pl.multiple_of(i * n, n)
        rows = pl.ds(r0, n)
        gm, bc = [], []
        for bb in range(nb):
            g = zm_ref[bb, rows, :] + gb_ref[...]
            lf = jnp.minimum(g, 0.0) - jnp.log(1.0 + jnp.exp(-jnp.abs(g)))
            gm.append(g)
            bc.append(jnp.dot(tril, lf, precision=lax.Precision.HIGHEST, preferred_element_type=F32))
        qk = [qk_ref[bb, rows, :] for bb in range(nb)]
        vv = [zv_ref[bb, rows, :] for bb in range(nb)]
        q = [qk[bb][:, h * C_DK:(h + 1) * C_DK] for bb, h in items]
        k = [qk[bb][:, CONV_CH // 2 + h * C_DK:CONV_CH // 2 + (h + 1) * C_DK] * (C_DK ** -0.5) for bb, h in items]
        vb = [vv[bb][:, h * C_DV:(h + 1) * C_DV].astype(BF16) for bb, h in items]
        qb = [x.astype(BF16) for x in q]
        c0 = [c_ref[bb, h] for bb, h in items]
        n0 = [n_ref[bb, h:h + 1, :] for bb, h in items]
        m0 = [m_ref[bb, h][:, 0:1] for bb, h in items]
        s_raw = [lax.dot_general(qb[t], k[t].astype(BF16), nt, preferred_element_type=F32) for t in range(len(items))]
        qc = [jnp.dot(qb[t], c0[t].astype(BF16), preferred_element_type=F32) for t in range(len(items))]
        gmt = [lax.dot_general(gate_sel, x, nt, precision=lax.Precision.HIGHEST, preferred_element_type=F32)
               for x in gm]
        bct = [lax.dot_general(gate_sel, x, nt, precision=lax.Precision.HIGHEST, preferred_element_type=F32)
               for x in bc]
        m, w_i, s = [], [], []
        for t, (bb, h) in enumerate(items):
            b_col = bc[bb][:, MISC_CF + h:MISC_CF + h + 1]
            b_row = bct[bb][C_HEADS + h:C_HEADS + h + 1, :]
            ig_row = gmt[bb][h:h + 1, :]
            d = jnp.where(causal, b_col - b_row + ig_row, NEG)
            inter = b_col + m0[t]
            mm = jnp.maximum(inter, jnp.max(d, axis=1, keepdims=True))
            m.append(mm)
            w_i.append(jnp.exp(inter - mm))
            s.append(s_raw[t] * jnp.exp(d - mm))
        sv = [jnp.dot(s[t].astype(BF16), vb[t], preferred_element_type=F32) for t in range(len(items))]
        ks, w_c, m_t = [], [], []
        for t, (bb, h) in enumerate(items):
            b_col = bc[bb][:, MISC_CF + h:MISC_CF + h + 1]
            ig_col = gm[bb][:, MISC_CI + h:MISC_CI + h + 1]
            mt = m[t][n - 1:n, :]
            b_last = b_col[n - 1:n, :]
            ks.append(k[t] * jnp.exp(b_last - b_col + ig_col - mt))
            w_c.append(jnp.exp(b_last + m0[t] - mt))
            m_t.append(mt)
        kst = [lax.dot_general(eye_k, x.astype(BF16), nt, preferred_element_type=F32).astype(BF16) for x in ks]
        upd = [jnp.dot(kst[t], vb[t], preferred_element_type=F32) for t in range(len(items))]
        zo = [zo_ref[bb, rows, :] for bb in range(nb)]
        for t, (bb, h) in enumerate(items):
            num = sv[t] + w_i[t] * qc[t]
            den = jnp.sum(s[t], axis=1, keepdims=True) + w_i[t] * jnp.sum(q[t] * n0[t], axis=1, keepdims=True)
            hh = num / jnp.maximum(jnp.abs(den), jnp.exp(-m[t]))
            c_ref[bb, h] = w_c[t] * c0[t] + upd[t]
            n_ref[bb, h:h + 1, :] = w_c[t] * n0[t] + jnp.sum(ks[t], axis=0, keepdims=True)
            m_ref[bb, h] = jnp.broadcast_to(m_t[t], (1, LANES))
            yn = hh * lax.rsqrt(jnp.mean(hh * hh, axis=-1, keepdims=True) + EPS) * gn_ref[...]
            hc_ref[bb, rows, h * C_DV:(h + 1) * C_DV] = yn * _sigmoid(zo[bb][:, h * C_DV:(h + 1) * C_DV])
        return carry

    lax.fori_loop(0, tb // n, chunk, 0)

    @pl.when(ci == pl.num_programs(1) - 1)
    def _():
        for bb in range(nb):
            c1_ref[bb] = c_ref[bb]
            n1_ref[bb] = n_ref[bb]
            m1_ref[bb] = jnp.concatenate([m_ref[bb, h][:, 0:1] for h in range(C_HEADS)], axis=1)


def _mlstm(z3, conv_w, conv_b, ig_b, fg_b, gn, c0, n0, m0, cv0):
    bsz, t, _ = z3.shape
    tb = _row_tile(t, 256)
    assert tb % CHUNK == 0
    width = 512
    gate_bias = jnp.zeros((1, LANES), F32)
    gate_bias = gate_bias.at[0, MISC_CI:MISC_CI + C_HEADS].set(ig_b.astype(F32))
    gate_bias = gate_bias.at[0, MISC_CF:MISC_CF + C_HEADS].set(fg_b.astype(F32))

    nb = MLSTM_BATCH if bsz % MLSTM_BATCH == 0 else 1

    def col(off, w=width):
        return pl.BlockSpec((nb, tb, w), lambda b, i: (b, i, off // w))

    def full(shape):
        nd = len(shape)
        return pl.BlockSpec(shape, lambda b, i: (0,) * nd)

    def per_b(shape):
        nd = len(shape)
        return pl.BlockSpec((nb,) + shape, lambda b, i: (b,) + (0,) * nd)

    return pl.pallas_call(
        functools.partial(_mlstm_kernel, tb=tb, nb=nb),
        grid=(bsz // nb, t // tb),
        in_specs=[col(COL_CQK), col(COL_CV), col(COL_CO), col(COL_MISC, LANES),
                  full((CONV_W, CONV_CH)), full((1, CONV_CH)), full((1, LANES)), full((1, C_DV)),
                  per_b((C_HEADS, C_DK, C_DV)), per_b((C_HEADS, C_DK)), per_b((1, C_HEADS)),
                  per_b((CONV_W - 1, CONV_CH))],
        out_specs=[pl.BlockSpec((nb, tb, width), lambda b, i: (b, i, 0)),
                   per_b((C_HEADS, C_DK, C_DV)), per_b((C_HEADS, C_DK)), per_b((1, C_HEADS))],
        out_shape=[jax.ShapeDtypeStruct((bsz, t, width), F32),
                   jax.ShapeDtypeStruct((bsz, C_HEADS, C_DK, C_DV), F32),
                   jax.ShapeDtypeStruct((bsz, C_HEADS, C_DK), F32),
                   jax.ShapeDtypeStruct((bsz, 1, C_HEADS), F32)],
        scratch_shapes=[pltpu.VMEM((nb, CONV_PAD + tb, CONV_CH), F32),
                        pltpu.VMEM((nb, tb, CONV_CH), F32),
                        pltpu.VMEM((nb, C_HEADS, C_DK, C_DV), F32),
                        pltpu.VMEM((nb, C_HEADS, C_DK), F32),
                        pltpu.VMEM((nb, C_HEADS, 1, LANES), F32)],
        compiler_params=_cparams("parallel", "arbitrary"),
        name="mlstm",
    )(z3, z3, z3, z3, conv_w, conv_b.reshape(1, CONV_CH), gate_bias, gn.reshape(1, C_DV),
      c0, n0, m0.reshape(bsz, 1, C_HEADS), cv0)


def _out_proj_kernel(oa_ref, ob_ref, hc_ref, x_ref, w_ref, o_ref):
    wa = w_ref[0:1024, :]
    wb = w_ref[1024:1536, :]
    wc = w_ref[1536:2048, :]
    acc = jnp.dot(oa_ref[...].astype(BF16), wa, preferred_element_type=F32)
    acc = acc + jnp.dot(ob_ref[...].astype(BF16), wb, preferred_element_type=F32)
    acc = acc + jnp.dot(hc_ref[...].astype(BF16), wc, preferred_element_type=F32)
    o_ref[...] = x_ref[...] + acc


def _out_proj(oa, ob, hc, x, w):
    n, d = x.shape
    tm = _row_tile(n, 512)
    tn = 1024
    return pl.pallas_call(
        _out_proj_kernel,
        grid=(n // tm, d // tn),
        in_specs=[pl.BlockSpec((tm, oa.shape[1]), lambda i, j: (i, 0)),
                  pl.BlockSpec((tm, ob.shape[1]), lambda i, j: (i, 0)),
                  pl.BlockSpec((tm, hc.shape[1]), lambda i, j: (i, 0)),
                  pl.BlockSpec((tm, tn), lambda i, j: (i, j)),
                  pl.BlockSpec((w.shape[0], tn), lambda i, j: (0, j))],
        out_specs=pl.BlockSpec((tm, tn), lambda i, j: (i, j)),
        out_shape=jax.ShapeDtypeStruct((n, d), F32),
        compiler_params=_cparams("parallel", "arbitrary"),
        name="out_proj",
    )(oa, ob, hc, x, w)


def _ffn_kernel(te_ref, nu_ref, x_ref, g_ref, wg_ref, wu_ref, wd_ref, o_ref, hn_ref, *, residual):
    i = pl.program_id(0)
    f = pl.program_id(1)

    @pl.when(i < nu_ref[0])
    def _():
        @pl.when(f == 0)
        def _():
            x = x_ref[...]
            y = x * lax.rsqrt(jnp.mean(x * x, axis=-1, keepdims=True) + EPS) * g_ref[...]
            hn_ref[...] = y.astype(BF16)
            o_ref[...] = x if residual else jnp.zeros_like(x)

        hn = hn_ref[...]
        a = jnp.dot(hn, wg_ref[0], preferred_element_type=F32)
        b = jnp.dot(hn, wu_ref[0], preferred_element_type=F32)
        hmid = (_silu(a) * b).astype(BF16)
        o_ref[...] += jnp.dot(hmid, wd_ref[0], preferred_element_type=F32)

    @pl.when((i >= nu_ref[0]) & (f == 0))
    def _():
        o_ref[...] = jnp.zeros_like(o_ref)


def _ffn(x, g, wg, wu, wd, tile_expert, n_used, tm, residual):
    p, d = x.shape
    ff = wg.shape[2]
    tf = 512
    assert ff % tf == 0 and p % tm == 0

    def w_in_map(i, f, te, nu):
        live = i < nu[0]
        return (te[i], 0, jnp.where(live, f, 0))

    def w_out_map(i, f, te, nu):
        live = i < nu[0]
        return (te[i], jnp.where(live, f, 0), 0)

    grid_spec = pltpu.PrefetchScalarGridSpec(
        num_scalar_prefetch=2,
        grid=(p // tm, ff // tf),
        in_specs=[pl.BlockSpec((tm, d), lambda i, f, te, nu: (i, 0)),
                  pl.BlockSpec((1, d), lambda i, f, te, nu: (0, 0)),
                  pl.BlockSpec((1, d, tf), w_in_map),
                  pl.BlockSpec((1, d, tf), w_in_map),
                  pl.BlockSpec((1, tf, d), w_out_map)],
        out_specs=pl.BlockSpec((tm, d), lambda i, f, te, nu: (i, 0)),
        scratch_shapes=[pltpu.VMEM((tm, d), BF16)],
    )
    return pl.pallas_call(
        functools.partial(_ffn_kernel, residual=residual),
        grid_spec=grid_spec,
        out_shape=jax.ShapeDtypeStruct((p, d), F32),
        compiler_params=_cparams("parallel", "arbitrary"),
        name="swiglu_ffn",
    )(tile_expert, n_used, x, g.reshape(1, d), wg, wu, wd)


def _router_kernel(x_ref, g_ref, w_ref, b_ref, e_ref, p_ref):
    x = x_ref[...]
    h = x * lax.rsqrt(jnp.mean(x * x, axis=-1, keepdims=True) + EPS) * g_ref[...]
    logits = jnp.dot(h, w_ref[...], precision=lax.Precision.HIGHEST, preferred_element_type=F32) + b_ref[...]
    lane = lax.broadcasted_iota(I32, logits.shape, 1)
    logits = jnp.where(lane < N_EXPERTS, logits, -jnp.inf)
    m1 = jnp.max(logits, axis=1, keepdims=True)
    i1 = jnp.min(jnp.where(logits == m1, lane, LANES), axis=1, keepdims=True)
    rest = jnp.where(lane == i1, -jnp.inf, logits)
    m2 = jnp.max(rest, axis=1, keepdims=True)
    i2 = jnp.min(jnp.where(rest == m2, lane, LANES), axis=1, keepdims=True)
    e2 = jnp.exp(m2 - m1)
    den = 1.0 + e2
    e_ref[...] = jnp.where(lane == 0, i1, jnp.where(lane == 1, i2, 0))
    p_ref[...] = jnp.where(lane == 0, 1.0 / den, jnp.where(lane == 1, e2 / den, 0.0))


def _router(x, g, w_router, b_router):
    n, d = x.shape
    tm = _row_tile(n, 512)
    w = jnp.zeros((d, LANES), F32).at[:, :N_EXPERTS].set(w_router.astype(F32))
    b = jnp.zeros((1, LANES), F32).at[0, :N_EXPERTS].set(b_router.astype(F32))
    return pl.pallas_call(
        _router_kernel,
        grid=(n // tm,),
        in_specs=[pl.BlockSpec((tm, d), lambda i: (i, 0)),
                  pl.BlockSpec((1, d), lambda i: (0, 0)),
                  pl.BlockSpec((d, LANES), lambda i: (0, 0)),
                  pl.BlockSpec((1, LANES), lambda i: (0, 0))],
        out_specs=[pl.BlockSpec((tm, LANES), lambda i: (i, 0)),
                   pl.BlockSpec((tm, LANES), lambda i: (i, 0))],
        out_shape=[jax.ShapeDtypeStruct((n, LANES), I32),
                   jax.ShapeDtypeStruct((n, LANES), F32)],
        compiler_params=_cparams("parallel"),
        name="moe_router",
    )(x, g.reshape(1, d), w, b)


def _gather_kernel(idx_ref, src_ref, o_ref, sem):
    rows = o_ref.shape[0]

    def row_copy(r):
        return pltpu.make_async_copy(src_ref.at[pl.ds(idx_ref[0, 0, r], 1), :], o_ref.at[pl.ds(r, 1), :], sem)

    def start(r, c):
        row_copy(r).start()
        return c

    def wait(r, c):
        row_copy(r).wait()
        return c

    lax.fori_loop(0, rows, start, 0)
    lax.fori_loop(0, rows, wait, 0)


def _gather_rows(src, idx):
    p = idx.shape[0]
    d = src.shape[1]
    rows = _row_tile(p, GATHER_ROWS)
    return pl.pallas_call(
        _gather_kernel,
        grid=(p // rows,),
        in_specs=[pl.BlockSpec((1, 1, rows), lambda i: (i, 0, 0), memory_space=pltpu.SMEM),
                  pl.BlockSpec(memory_space=pl.ANY)],
        out_specs=pl.BlockSpec((rows, d), lambda i: (i, 0)),
        out_shape=jax.ShapeDtypeStruct((p, d), src.dtype),
        scratch_shapes=[pltpu.SemaphoreType.DMA(())],
        compiler_params=_cparams("arbitrary"),
        name="row_gather",
    )(idx.reshape(p // rows, 1, rows), src)


def _combine_kernel(x_ref, r1_ref, r2_ref, p_ref, o_ref):
    p = p_ref[...]
    o_ref[...] = x_ref[...] + (r1_ref[...] * p[:, 0:1] + r2_ref[...] * p[:, 1:2])


def _combine(x, r1, r2, gates):
    n, d = x.shape
    tm = _row_tile(n, 512)
    row = pl.BlockSpec((tm, d), lambda i: (i, 0))
    return pl.pallas_call(
        _combine_kernel,
        grid=(n // tm,),
        in_specs=[row, row, row, pl.BlockSpec((tm, LANES), lambda i: (i, 0))],
        out_specs=row,
        out_shape=jax.ShapeDtypeStruct((n, d), F32),
        compiler_params=_cparams("parallel"),
        name="moe_combine",
    )(x, r1, r2, gates)


def _moe(x, g, w_router, b_router, wg, wu, wd):
    n, d = x.shape
    e_out, gates = _router(x, g, w_router, b_router)
    flat_e = e_out[:, :TOP_K].reshape(n * TOP_K)
    nk = n * TOP_K
    tm = 512 if nk >= 8192 else 128
    onehot = (flat_e[:, None] == jnp.arange(N_EXPERTS, dtype=I32)[None, :]).astype(I32)
    csum = jnp.cumsum(onehot, axis=0)
    counts = csum[-1]
    rank = jnp.sum((csum - 1) * onehot, axis=1)
    padded = (counts + tm - 1) // tm * tm
    pad_end = jnp.cumsum(padded)
    pad_start = pad_end - padded
    dest = (pad_start[flat_e] + rank).astype(I32)
    n_tiles = (nk + N_EXPERTS * (tm - 1) + tm - 1) // tm
    p = n_tiles * tm
    src_tok = jnp.zeros((p,), I32).at[dest].set(jnp.arange(nk, dtype=I32) // TOP_K)
    tile_expert = jnp.minimum(jnp.searchsorted(pad_end, jnp.arange(n_tiles, dtype=I32) * tm, side='right'),
                              N_EXPERTS - 1).astype(I32)
    n_used = (pad_end[-1] // tm).astype(I32).reshape(1)

    xs = _gather_rows(x, src_tok)
    yb = _ffn(xs, g, wg, wu, wd, tile_expert, n_used, tm, residual=False)
    dest2 = dest.reshape(n, TOP_K)
    r1 = _gather_rows(yb, dest2[:, 0])
    r2 = _gather_rows(yb, dest2[:, 1])
    return _combine(x, r1, r2, gates)


def _reorder_w_in(w):
    offs, o = {}, 0
    for name, width in (("a_q", 1024), ("a_k", 256), ("a_v", 256), ("i_q", 512), ("i_w", 8), ("i_k", 64),
                        ("b_q", 512), ("b_f", 512), ("b_i", 512), ("b_g", 512), ("c_qk", 512), ("c_v", 512),
                        ("c_i", 4), ("c_f", 4), ("c_o", 512)):
        offs[name] = (o, width)
        o += width

    def c(name):
        s, width = offs[name]
        return w[:, s:s + width]

    pad = jnp.zeros((w.shape[0], LANES - (IDX_DIM + IDX_HEADS + 2 * C_HEADS)), w.dtype)
    cols = [c("a_q"), c("a_k"), c("a_v"), c("i_q"), c("b_q"), c("b_f"), c("b_i"), c("b_g"),
            c("c_qk"), c("c_v"), c("c_o"), c("i_k"), c("i_w"), c("c_i"), c("c_f"), pad]
    return jnp.concatenate(cols, axis=1).astype(BF16)


def _trunk(x, pos0, W, lbs, past):
    bsz, t, d = x.shape
    n = bsz * t
    pos = pos0 + jnp.arange(t, dtype=I32)
    xf = x.reshape(n, d)
    states = []
    depth = W['w_in'].shape[0]
    for l in range(depth):
        z = _norm_matmul(xf, W['attn_norm'][l], W['w_in_r'][l], tn=1152)
        z3 = z.reshape(bsz, t, Z_WIDTH)
        q, k, qi, ki = _prep(z3, W['q_norm'][l], W['k_norm'][l], pos)
        v = z3[:, :, COL_AV:COL_AV + 256]
        if past is None:
            n_keys = t
            assert t % KEY_BLOCK == 0
            o_a = _sparse_attention(q, qi, z3, k, z3, COL_AV // 256, ki, n_keys, 0)
            s0 = jnp.zeros((bsz, B_HEADS, B_DK, B_DV), F32)
            c0 = jnp.zeros((bsz, C_HEADS, C_DK, C_DV), F32)
            n0 = jnp.zeros((bsz, C_HEADS, C_DK), F32)
            m0 = jnp.zeros((bsz, C_HEADS), F32)
            cv0 = jnp.zeros((bsz, CONV_W - 1, CONV_CH), F32)
        else:
            pk = past['k'][l].reshape(bsz, -1, 256)
            pv = past['v'][l].reshape(bsz, -1, 256)
            n_keys = pk.shape[1] + t
            lp = (n_keys + KEY_BLOCK - 1) // KEY_BLOCK * KEY_BLOCK
            padk = jnp.zeros((bsz, lp - n_keys, 256), F32)
            k_all = jnp.concatenate([pk, k, padk], axis=1)
            v_all = jnp.concatenate([pv, v, padk], axis=1)
            ki_all = jnp.concatenate([past['ki'][l], ki, padk[:, :, :IDX_DIM]], axis=1)
            o_a = _sparse_attention(q, qi, z3, k_all, v_all, 0, ki_all, n_keys, pk.shape[1])
            s0, c0, n0, m0, cv0 = past['hgrn'][l], past['C'][l], past['n'][l], past['m'][l], past['conv'][l]
        ob, s1 = _hgrn(z3, lbs[l], W['hgrn_norm'][l], s0)
        hc, c1, n1, m1 = _mlstm(z3, W['conv_w'][l], W['conv_b'][l], W['ig_b'][l], W['fg_b'][l],
                                W['mlstm_norm'][l], c0, n0, m0, cv0)
        cqk = z3[:, :, COL_CQK:COL_CQK + CONV_CH]
        conv1 = jnp.concatenate([cv0, cqk], axis=1)[:, -(CONV_W - 1):]
        xf = _out_proj(o_a.reshape(n, -1), ob.reshape(n, -1), hc.reshape(n, -1), xf, W['w_out_b'][l])
        jj = l // 2
        if l % 2 == 0:
            tm = _row_tile(n, 512)
            te = jnp.zeros((n // tm,), I32)
            nu = jnp.full((1,), n // tm, I32)
            xf = _ffn(xf, W['ffn_norm'][l], W['dense_wg_b'][jj][None], W['dense_wu_b'][jj][None],
                      W['dense_wd_b'][jj][None], te, nu, tm, residual=True)
        else:
            xf = _moe(xf, W['ffn_norm'][l], W['moe_router'][jj], W['moe_router_b'][jj],
                      W['moe_wg_b'][jj], W['moe_wu_b'][jj], W['moe_wd_b'][jj])
        states.append(dict(k=k.reshape(bsz, t, A_KV_HEADS, HEAD_DIM), v=v.reshape(bsz, t, A_KV_HEADS, HEAD_DIM),
                           ki=ki, hgrn=s1, C=c1, n=n1, m=m1.reshape(bsz, C_HEADS), conv=conv1))
    stacked = {name: jnp.stack([s[name] for s in states]) for name in states[0]}
    return xf.reshape(bsz, t, d), stacked


def kernel(x_prompt, x_sample, cache_k, cache_v, cache_kidx, state_hgrn, state_mlstm_C, state_mlstm_n,
           state_mlstm_m, state_mlstm_conv, attn_norm, w_in, w_out, q_norm, k_norm, hgrn_lb_logits,
           hgrn_norm, conv_w, conv_b, ig_b, fg_b, mlstm_norm, ffn_norm, dense_wg, dense_wu, dense_wd,
           moe_router, moe_router_b, moe_wg, moe_wu, moe_wd):
    depth = w_in.shape[0]
    W = dict(attn_norm=attn_norm, w_in=w_in, q_norm=q_norm, k_norm=k_norm, hgrn_norm=hgrn_norm,
             conv_w=conv_w, conv_b=conv_b, ig_b=ig_b, fg_b=fg_b, mlstm_norm=mlstm_norm, ffn_norm=ffn_norm,
             moe_router=moe_router, moe_router_b=moe_router_b,
             w_in_r=jnp.stack([_reorder_w_in(w_in[l]) for l in range(depth)]),
             w_out_b=w_out.astype(BF16),
             dense_wg_b=dense_wg.astype(BF16), dense_wu_b=dense_wu.astype(BF16), dense_wd_b=dense_wd.astype(BF16),
             moe_wg_b=moe_wg.astype(BF16), moe_wu_b=moe_wu.astype(BF16), moe_wd_b=moe_wd.astype(BF16))
    p_lb = jax.nn.softmax(hgrn_lb_logits.astype(F32), axis=0)
    lbs = jnp.cumsum(p_lb, axis=0) - p_lb[0]
    y_prompt, sp = _trunk(x_prompt, 0, W, lbs, None)
    past = dict(k=cache_k, v=cache_v, ki=cache_kidx, hgrn=state_hgrn, C=state_mlstm_C, n=state_mlstm_n,
                m=state_mlstm_m, conv=state_mlstm_conv)
    y_sample, ss = _trunk(x_sample, cache_k.shape[2], W, lbs, past)
    return (y_prompt, y_sample,
            sp['k'], sp['v'], sp['ki'], sp['hgrn'], sp['C'], sp['n'], sp['m'], sp['conv'],
            ss['k'], ss['v'], ss['ki'], ss['hgrn'], ss['C'], ss['n'], ss['m'], ss['conv'])
```

```python
import functools
import math

import numpy as np
import jax
import jax.numpy as jnp
from jax import lax
from jax.experimental import pallas as pl
from jax.experimental.pallas import tpu as pltpu

F32 = jnp.float32
BF16 = jnp.bfloat16
I32 = jnp.int32

CHUNK = 64
EPS = 1e-6
NEG = -1e30
HEAD_DIM = 128
A_HEADS = 8
A_KV_HEADS = 2
GQA = A_HEADS // A_KV_HEADS
IDX_HEADS = 8
IDX_DIM = 64
TOPK_MAX = 256
ROPE_THETA = 500000.0
ROT_FRAC = 4
ATTN_SCALE = HEAD_DIM ** -0.5
IDX_SCALE = (IDX_HEADS * IDX_DIM) ** -0.5
B_HEADS = 4
B_DK = 128
B_DV = 128
C_HEADS = 4
C_DK = 64
C_DV = 128
CONV_W = 4
CONV_CH = 2 * C_HEADS * C_DK
N_EXPERTS = 8
TOP_K = 2

COL_AQ, COL_AK, COL_AV, COL_IQ = 0, 1024, 1280, 1536
COL_BQ, COL_BF, COL_BI, COL_BG = 2048, 2560, 3072, 3584
COL_CQK, COL_CV, COL_CO, COL_MISC = 4096, 4608, 5120, 5632
Z_WIDTH = 5760
MISC_IK, MISC_IW, MISC_CI, MISC_CF = 0, 64, 72, 76

LANES = 128
VMEM_LIMIT_BYTES = 56 * 1024 * 1024
INT_MIN = -2 ** 31

HGRN_SUB = 8
KEY_BLOCK = 512
ATTN_ROWS = 256
GATHER_ROWS = 256
FFN_COLS = 512
MLSTM_BATCH = 2


def _cparams(*sem):
    return pltpu.CompilerParams(dimension_semantics=sem, vmem_limit_bytes=VMEM_LIMIT_BYTES)


def _sigmoid(x):
    return 1.0 / (1.0 + jnp.exp(-x))


def _silu(x):
    return x * _sigmoid(x)


def _row_tile(n, pref):
    t = min(n, pref)
    while n % t:
        t //= 2
    return t


def _norm_matmul_kernel(x_ref, g_ref, w_ref, o_ref, xn_ref):
    @pl.when(pl.program_id(1) == 0)
    def _():
        x = x_ref[...]
        y = x * lax.rsqrt(jnp.mean(x * x, axis=-1, keepdims=True) + EPS) * g_ref[...]
        xn_ref[...] = y.astype(BF16)

    o_ref[...] = jnp.dot(xn_ref[...], w_ref[...], preferred_element_type=F32)


def _norm_matmul(x, g, w, tn):
    n, d = x.shape
    nout = w.shape[1]
    tm = _row_tile(n, 1024)
    return pl.pallas_call(
        _norm_matmul_kernel,
        grid=(n // tm, nout // tn),
        in_specs=[pl.BlockSpec((tm, d), lambda i, j: (i, 0)),
                  pl.BlockSpec((1, d), lambda i, j: (0, 0)),
                  pl.BlockSpec((d, tn), lambda i, j: (0, j))],
        out_specs=pl.BlockSpec((tm, tn), lambda i, j: (i, j)),
        out_shape=jax.ShapeDtypeStruct((n, nout), F32),
        scratch_shapes=[pltpu.VMEM((tm, d), BF16)],
        compiler_params=_cparams("parallel", "arbitrary"),
        name="norm_in_proj",
    )(x, g.reshape(1, d), w)


def _rope(x, c, sa, sb, half):
    w = x.shape[-1]
    return x * c + pltpu.roll(x, w - half, 1) * sa + pltpu.roll(x, half, 1) * sb


def _prep_kernel(zq_ref, zk_ref, ziq_ref, zm_ref, qn_ref, kn_ref, t128_ref, t64_ref,
                 q_ref, k_ref, qi_ref, ki_ref):
    c1, sa1, sb1 = t128_ref[0], t128_ref[1], t128_ref[2]
    c2, sa2, sb2 = t64_ref[0], t64_ref[1], t64_ref[2]
    half1 = HEAD_DIM // ROT_FRAC // 2
    half2 = IDX_DIM // ROT_FRAC // 2

    def normed(x, g):
        return x * lax.rsqrt(jnp.mean(x * x, axis=-1, keepdims=True) + EPS) * g

    for h in range(A_HEADS):
        x = zq_ref[0, :, h * HEAD_DIM:(h + 1) * HEAD_DIM]
        q_ref[0, h] = _rope(normed(x, qn_ref[...]), c1, sa1, sb1, half1)
    for h in range(A_KV_HEADS):
        x = zk_ref[0, :, h * HEAD_DIM:(h + 1) * HEAD_DIM]
        k_ref[0, :, h * HEAD_DIM:(h + 1) * HEAD_DIM] = _rope(normed(x, kn_ref[...]), c1, sa1, sb1, half1)
    for j in range(IDX_HEADS * IDX_DIM // LANES):
        x = ziq_ref[0, :, j * LANES:(j + 1) * LANES]
        qi_ref[0, :, j * LANES:(j + 1) * LANES] = _rope(x, c2, sa2, sb2, half2)
    m = _rope(zm_ref[0], c2, sa2, sb2, half2)
    ki_ref[0] = m[:, MISC_IK:MISC_IK + IDX_DIM]


def _rope_tables(pos, period):
    rot = period // ROT_FRAC
    half = rot // 2
    inv = jnp.exp(-math.log(ROPE_THETA) * 2.0 * jnp.arange(half, dtype=F32) / rot)
    ang = pos.astype(F32)[:, None] * inv
    cos, sin = jnp.cos(ang), jnp.sin(ang)
    t = pos.shape[0]
    ones = jnp.ones((t, period - rot), F32)
    zeros_h = jnp.zeros((t, half), F32)
    zeros_r = jnp.zeros((t, period - rot), F32)
    c = jnp.concatenate([cos, cos, ones], axis=1)
    sa = jnp.concatenate([-sin, zeros_h, zeros_r], axis=1)
    sb = jnp.concatenate([zeros_h, sin, zeros_r], axis=1)
    tab = jnp.stack([c, sa, sb])
    return jnp.tile(tab, (1, 1, LANES // period))


def _prep(z3, q_norm, k_norm, pos):
    bsz, t, _ = z3.shape
    tt = _row_tile(t, 256)
    t128 = _rope_tables(pos, HEAD_DIM)
    t64 = _rope_tables(pos, IDX_DIM)

    def col(width, off):
        return pl.BlockSpec((1, tt, width), lambda b, i: (b, i, off // width))

    return pl.pallas_call(
        _prep_kernel,
        grid=(bsz, t // tt),
        in_specs=[col(1024, COL_AQ), col(256, COL_AK), col(512, COL_IQ), col(128, COL_MISC),
                  pl.BlockSpec((1, HEAD_DIM), lambda b, i: (0, 0)),
                  pl.BlockSpec((1, HEAD_DIM), lambda b, i: (0, 0)),
                  pl.BlockSpec((3, tt, LANES), lambda b, i: (0, i, 0)),
                  pl.BlockSpec((3, tt, LANES), lambda b, i: (0, i, 0))],
        out_specs=[pl.BlockSpec((1, A_HEADS, tt, HEAD_DIM), lambda b, i: (b, 0, i, 0)),
                   pl.BlockSpec((1, tt, A_KV_HEADS * HEAD_DIM), lambda b, i: (b, i, 0)),
                   pl.BlockSpec((1, tt, IDX_HEADS * IDX_DIM), lambda b, i: (b, i, 0)),
                   pl.BlockSpec((1, tt, IDX_DIM), lambda b, i: (b, i, 0))],
        out_shape=[jax.ShapeDtypeStruct((bsz, A_HEADS, t, HEAD_DIM), F32),
                   jax.ShapeDtypeStruct((bsz, t, A_KV_HEADS * HEAD_DIM), F32),
                   jax.ShapeDtypeStruct((bsz, t, IDX_HEADS * IDX_DIM), F32),
                   jax.ShapeDtypeStruct((bsz, t, IDX_DIM), F32)],
        compiler_params=_cparams("parallel", "parallel"),
        name="qk_prep",
    )(z3, z3, z3, z3, q_norm.reshape(1, HEAD_DIM), k_norm.reshape(1, HEAD_DIM), t128, t64)


def _f2key(x):
    bits = lax.bitcast_convert_type(x, I32)
    return jnp.where(bits < 0, bits ^ 0x7FFFFFFF, bits)


def _attn_kernel(q_ref, qi_ref, zm_ref, k_ref, v_ref, ki_ref, o_ref,
                 key_ref, bias_ref, qs_ref, kb_ref, vb_ref, kib_ref, acc_ref, m_ref, l_ref, *,
                 tq, n_keys, topk, q_pos0, kb_size, lp):
    j = pl.program_id(1)
    q_first = q_pos0 + j * tq
    lim = jnp.minimum(((q_first + tq - 1) // CHUNK + 1) * CHUNK, n_keys)
    n_kb = (lim + kb_size - 1) // kb_size

    @pl.when(j == 0)
    def _():
        def cast_body(kb, c):
            sl = pl.ds(pl.multiple_of(kb * kb_size, kb_size), kb_size)
            kb_ref[sl, :] = k_ref[0, sl, :].astype(BF16)
            vb_ref[sl, :] = v_ref[0, sl, :].astype(BF16)
            kib_ref[sl, :] = ki_ref[0, sl, :].astype(BF16)
            return c
        lax.fori_loop(0, lp // kb_size, cast_body, 0)

    for h in range(A_HEADS):
        qs_ref[h // GQA, (h % GQA) * tq:(h % GQA + 1) * tq, :] = \
            (q_ref[0, h] * (ATTN_SCALE * math.log2(math.e))).astype(BF16)

    row = lax.broadcasted_iota(I32, (tq, kb_size), 0)
    lane = lax.broadcasted_iota(I32, (tq, kb_size), 1)
    q_chunk = lax.shift_right_logical(q_first + row, 6)

    qi = qi_ref[0].astype(BF16)
    wi = zm_ref[0][:, MISC_IW:MISC_IW + IDX_HEADS] * IDX_SCALE

    def score_body(kb, carry):
        k0 = pl.multiple_of(kb * kb_size, kb_size)
        ki_blk = kib_ref[pl.ds(k0, kb_size), :]
        s = jnp.zeros((tq, kb_size), F32)
        for h in range(IDX_HEADS):
            r = lax.dot_general(qi[:, h * IDX_DIM:(h + 1) * IDX_DIM], ki_blk,
                                (((1,), (1,)), ((), ())), preferred_element_type=F32)
            s = s + jnp.maximum(r, 0.0) * wi[:, h:h + 1]
        kidx = k0 + lane
        admiss = (lax.shift_right_logical(kidx, 6) <= q_chunk) & (kidx < n_keys)
        key_ref[kb] = _f2key(jnp.where(admiss, s, NEG))
        return carry

    lax.fori_loop(0, n_kb, score_body, 0)

    def count(pred_fn):
        def body(kb, acc):
            m = pred_fn(key_ref[kb], kb).astype(I32)
            for c in range(kb_size // LANES):
                acc = acc + m[:, c * LANES:(c + 1) * LANES]
            return acc
        acc = lax.fori_loop(0, n_kb, body, jnp.zeros((tq, LANES), I32))
        return jnp.sum(acc, axis=1, keepdims=True)

    q_chunk1 = lax.shift_right_logical(q_first + lax.broadcasted_iota(I32, (tq, 1), 0), 6)
    n_adm = jnp.minimum((q_chunk1 + 1) * CHUNK, n_keys)
    neg_key = int(np.float32(NEG).view(np.int32)) ^ 0x7FFFFFFF
    few = n_adm <= topk

    def search_cond(st):
        step, prefix, thr, c_thr, done = st
        return (step < 32) & (jnp.min(done) == 0)

    def search_body(st):
        step, prefix, thr, c_thr, done = st
        cand = prefix ^ lax.shift_left(jnp.int32(1), 31 - step)
        c = count(lambda key, kb: key >= cand)
        ge = c >= topk
        hit = (c == topk) & (done == 0)
        return (step + 1, jnp.where(ge, cand, prefix), jnp.where(hit, cand, thr),
                jnp.where(hit, c, c_thr), jnp.where(hit, 1, done))

    init = (jnp.int32(0), jnp.full((tq, 1), INT_MIN, I32), jnp.full((tq, 1), neg_key, I32),
            jnp.full((tq, 1), topk, I32), few.astype(I32))
    step_f, prefix_f, thr, c_ge, done = lax.while_loop(search_cond, search_body, init)
    thr = jnp.where(done != 0, thr, prefix_f)
    c_ge = jnp.where(done != 0, c_ge, topk + 1)

    def tie_limit():
        need = topk - count(lambda key, kb: key > thr)

        def idx_body(i, x):
            cand = x | lax.shift_left(jnp.int32(1), 14 - i)
            c = count(lambda key, kb: (key == thr) & (kb * kb_size + lane < cand))
            return jnp.where(c < need, cand, x)
        return lax.fori_loop(0, 15, idx_body, jnp.zeros((tq, 1), I32))

    xlim = lax.cond(jnp.max(c_ge) > topk, tie_limit, lambda: jnp.full((tq, 1), 2 ** 30, I32))

    valid_key = int(np.float32(0.5 * NEG).view(np.int32)) ^ 0x7FFFFFFF

    def mask_body(kb, carry):
        key = key_ref[kb]
        sel = (key > thr) | ((key == thr) & (kb * kb_size + lane <= xlim))
        bias_ref[kb] = jnp.where(sel & (key > valid_key), 0.0, NEG)
        return carry

    lax.fori_loop(0, n_kb, mask_body, 0)

    n_col = kb_size // LANES

    def logits(g, kb):
        sl = pl.ds(pl.multiple_of(kb * kb_size, kb_size), kb_size)
        k_blk = kb_ref[sl, g * HEAD_DIM:(g + 1) * HEAD_DIM]
        s = lax.dot_general(qs_ref[g], k_blk, (((1,), (1,)), ((), ())), preferred_element_type=F32)
        return s.reshape(GQA, tq, kb_size) + bias_ref[kb][None]

    for g in range(A_KV_HEADS):
        def max_body(kb, mx, g=g):
            s = logits(g, kb)
            for c in range(n_col):
                mx = jnp.maximum(mx, s[:, :, c * LANES:(c + 1) * LANES])
            return mx

        mx = lax.fori_loop(0, n_kb, max_body, jnp.full((GQA, tq, LANES), NEG, F32))
        m_ref[...] = jnp.broadcast_to(jnp.max(mx, axis=2, keepdims=True), (GQA, tq, LANES))
        l_ref[...] = jnp.zeros(l_ref.shape, F32)
        acc_ref[...] = jnp.zeros(acc_ref.shape, F32)

        def att_body(kb, carry, g=g):
            sl = pl.ds(pl.multiple_of(kb * kb_size, kb_size), kb_size)
            v_blk = vb_ref[sl, g * HEAD_DIM:(g + 1) * HEAD_DIM]
            mb = m_ref[...]
            p = jnp.exp2(logits(g, kb) - jnp.concatenate([mb] * n_col, axis=2))
            lsum = l_ref[...]
            for c in range(n_col):
                lsum = lsum + p[:, :, c * LANES:(c + 1) * LANES]
            l_ref[...] = lsum
            pv = jnp.dot(p.reshape(GQA * tq, kb_size).astype(BF16), v_blk, preferred_element_type=F32)
            acc_ref[...] += pv.reshape(GQA, tq, HEAD_DIM)
            return carry

        lax.fori_loop(0, n_kb, att_body, 0)
        out = acc_ref[...] / jnp.sum(l_ref[...], axis=2, keepdims=True)
        for hh in range(GQA):
            h = g * GQA + hh
            o_ref[0, :, h * HEAD_DIM:(h + 1) * HEAD_DIM] = out[hh].astype(BF16)


def _sparse_attention(q, qi, z3, k_all, v_src, v_col, ki_all, n_keys, q_pos0):
    bsz, _, t, _ = q.shape
    lp = k_all.shape[1]
    tq = _row_tile(t, ATTN_ROWS)
    topk = min(TOPK_MAX, n_keys // 4)
    kb_size = KEY_BLOCK
    assert lp % kb_size == 0 and kb_size >= topk
    kern = functools.partial(_attn_kernel, tq=tq, n_keys=n_keys, topk=topk, q_pos0=q_pos0, kb_size=kb_size,
                             lp=lp)
    return pl.pallas_call(
        kern,
        grid=(bsz, t // tq),
        in_specs=[pl.BlockSpec((1, A_HEADS, tq, HEAD_DIM), lambda b, i: (b, 0, i, 0)),
                  pl.BlockSpec((1, tq, IDX_HEADS * IDX_DIM), lambda b, i: (b, i, 0)),
                  pl.BlockSpec((1, tq, LANES), lambda b, i: (b, i, COL_MISC // LANES)),
                  pl.BlockSpec((1, lp, 256), lambda b, i: (b, 0, 0)),
                  pl.BlockSpec((1, lp, 256), lambda b, i: (b, 0, v_col)),
                  pl.BlockSpec((1, lp, IDX_DIM), lambda b, i: (b, 0, 0))],
        out_specs=pl.BlockSpec((1, tq, A_HEADS * HEAD_DIM), lambda b, i: (b, i, 0)),
        out_shape=jax.ShapeDtypeStruct((bsz, t, A_HEADS * HEAD_DIM), BF16),
        scratch_shapes=[pltpu.VMEM((lp // kb_size, tq, kb_size), I32),
                        pltpu.VMEM((lp // kb_size, tq, kb_size), F32),
                        pltpu.VMEM((A_KV_HEADS, GQA * tq, HEAD_DIM), BF16),
                        pltpu.VMEM((lp, A_KV_HEADS * HEAD_DIM), BF16),
                        pltpu.VMEM((lp, A_KV_HEADS * HEAD_DIM), BF16),
                        pltpu.VMEM((lp, IDX_DIM), BF16),
                        pltpu.VMEM((GQA, tq, HEAD_DIM), F32),
                        pltpu.VMEM((GQA, tq, LANES), F32),
                        pltpu.VMEM((GQA, tq, LANES), F32)],
        compiler_params=_cparams("parallel", "arbitrary"),
        name="sparse_attention",
    )(q, qi, z3, k_all, v_src, ki_all)


def _hgrn_kernel(zq_ref, zf_ref, zi_ref, zg_ref, lb_ref, gn_ref, s0_ref, ob_ref, s1_ref, st_ref, *, tb):
    c = pl.program_id(1)
    sub = HGRN_SUB

    @pl.when(c == 0)
    def _():
        for h in range(B_HEADS):
            st_ref[h] = s0_ref[0, h].T

    n = CHUNK
    ns = n // sub
    rr = lax.broadcasted_iota(I32, (n, n), 0)
    cc = lax.broadcasted_iota(I32, (n, n), 1)
    tril = (rr >= cc).astype(F32)
    rowi = lax.broadcasted_iota(I32, (n, B_DK), 0)
    t_in = lax.broadcasted_iota(I32, (ns, sub, B_DK), 1)
    nt = (((1,), (1,)), ((), ()))

    def step(i, carry):
        r0 = pl.multiple_of(i * n, n)
        zq = zq_ref[0, pl.ds(r0, n), :]
        zf = zf_ref[0, pl.ds(r0, n), :]
        zi = zi_ref[0, pl.ds(r0, n), :]
        zg = zg_ref[0, pl.ds(r0, n), :]
        for h in range(B_HEADS):
            sl = slice(h * B_DK, (h + 1) * B_DK)
            lb = lb_ref[:, sl]
            f = lb + (1.0 - lb) * _sigmoid(zf[:, sl])
            logf = jnp.log(jnp.maximum(f, 1e-30))
            k = 1.0 - f
            q = _silu(zq[:, sl])
            v = zi[:, sl]
            vb = v.astype(BF16)
            b = jnp.dot(tril, logf, precision=lax.Precision.HIGHEST, preferred_element_type=F32)
            b_last = b[n - 1:n, :]
            st = st_ref[h]
            o = lax.dot_general((q * jnp.exp(b)).astype(BF16), st.astype(BF16), nt,
                                preferred_element_type=F32)
            strips = [jnp.zeros((sub, n), F32)]
            for j in range(1, ns):
                ref = b[j * sub - 1:j * sub, :]
                qt = q[j * sub:(j + 1) * sub] * jnp.exp(b[j * sub:(j + 1) * sub] - ref)
                kt = k * jnp.exp(jnp.where(rowi < j * sub, ref - b, NEG))
                strips.append(lax.dot_general(qt.astype(BF16), kt.astype(BF16), nt,
                                              preferred_element_type=F32))
            o = o + jnp.dot(jnp.concatenate(strips, axis=0).astype(BF16), vb, preferred_element_type=F32)
            b3 = b.reshape(ns, sub, B_DK)
            q3 = q.reshape(ns, sub, B_DK)
            k3 = k.reshape(ns, sub, B_DK)
            v3 = v.reshape(ns, sub, B_DV)
            od = jnp.zeros((ns, sub, B_DV), F32)
            for s in range(sub):
                w = jnp.exp(jnp.where(t_in >= s, b3 - b3[:, s:s + 1, :], NEG))
                a = jnp.sum(q3 * w * k3[:, s:s + 1, :], axis=-1, keepdims=True)
                od = od + a * v3[:, s:s + 1, :]
            o = o + od.reshape(n, B_DV)
            y = o * lax.rsqrt(jnp.mean(o * o, axis=-1, keepdims=True) + EPS) * gn_ref[...]
            ob_ref[0, pl.ds(r0, n), sl] = (y * _silu(zg[:, sl])).astype(BF16)
            kt = k * jnp.exp(b_last - b)
            upd = lax.dot_general(vb, kt.astype(BF16), (((0,), (0,)), ((), ())),
                                  preferred_element_type=F32)
            st_ref[h] = st * jnp.exp(b_last) + upd
        return carry

    lax.fori_loop(0, tb // n, step, 0)

    @pl.when(c == pl.num_programs(1) - 1)
    def _():
        for h in range(B_HEADS):
            s1_ref[0, h] = st_ref[h].T


def _hgrn(z3, lb, gn, s0):
    bsz, t, _ = z3.shape
    tb = _row_tile(t, 256)
    width = B_HEADS * B_DK

    def col(off):
        return pl.BlockSpec((1, tb, width), lambda b, i: (b, i, off // width))

    return pl.pallas_call(
        functools.partial(_hgrn_kernel, tb=tb),
        grid=(bsz, t // tb),
        in_specs=[col(COL_BQ), col(COL_BF), col(COL_BI), col(COL_BG),
                  pl.BlockSpec((1, width), lambda b, i: (0, 0)),
                  pl.BlockSpec((1, B_DV), lambda b, i: (0, 0)),
                  pl.BlockSpec((1, B_HEADS, B_DK, B_DV), lambda b, i: (b, 0, 0, 0))],
        out_specs=[pl.BlockSpec((1, tb, width), lambda b, i: (b, i, 0)),
                   pl.BlockSpec((1, B_HEADS, B_DK, B_DV), lambda b, i: (b, 0, 0, 0))],
        out_shape=[jax.ShapeDtypeStruct((bsz, t, width), BF16),
                   jax.ShapeDtypeStruct((bsz, B_HEADS, B_DK, B_DV), F32)],
        scratch_shapes=[pltpu.VMEM((B_HEADS, B_DV, B_DK), F32)],
        compiler_params=_cparams("parallel", "arbitrary"),
        name="hgrn2",
    )(z3, z3, z3, z3, lb.reshape(1, width), gn.reshape(1, B_DV), s0)


CONV_PAD = 8


def _mlstm_kernel(zqk_ref, zv_ref, zo_ref, zm_ref, cw_ref, cb_ref, gb_ref, gn_ref,
                  c0_ref, n0_ref, m0_ref, cv0_ref,
                  hc_ref, c1_ref, n1_ref, m1_ref,
                  xe_ref, qk_ref, c_ref, n_ref, m_ref, *, tb, nb):
    ci = pl.program_id(1)
    keep = CONV_W - 1

    @pl.when(ci == 0)
    def _():
        for bb in range(nb):
            xe_ref[bb, CONV_PAD - keep:CONV_PAD, :] = cv0_ref[bb]
            c_ref[bb] = c0_ref[bb]
            n_ref[bb] = n0_ref[bb]
            m0_all = m0_ref[bb]
            for h in range(C_HEADS):
                m_ref[bb, h] = jnp.broadcast_to(m0_all[:, h:h + 1], (1, LANES))

    for bb in range(nb):
        xe_ref[bb, CONV_PAD:CONV_PAD + tb, :] = zqk_ref[bb]
        y = cb_ref[...]
        for jw in range(CONV_W):
            y = y + cw_ref[jw:jw + 1, :] * xe_ref[bb, CONV_PAD - keep + jw:CONV_PAD - keep + jw + tb, :]
        qk_ref[bb] = _silu(y)
        xe_ref[bb, CONV_PAD - keep:CONV_PAD, :] = zqk_ref[bb, tb - keep:tb, :]

    n = CHUNK
    rr = lax.broadcasted_iota(I32, (n, n), 0)
    cc = lax.broadcasted_iota(I32, (n, n), 1)
    causal = rr >= cc
    tril = causal.astype(F32)
    eye_k = (rr == cc).astype(BF16)
    gate_sel = (lax.broadcasted_iota(I32, (2 * C_HEADS, LANES), 1) ==
                lax.broadcasted_iota(I32, (2 * C_HEADS, LANES), 0) + MISC_CI).astype(F32)

    items = [(bb, h) for bb in range(nb) for h in range(C_HEADS)]
    nt = (((1,), (1,)), ((), ()))

    def chunk(i, carry):
        r0 = pl.multiple_of(i * n, n)
        rows = pl.ds(r0, n)
        gm, bc = [], []
        for bb in range(nb):
            g = zm_ref[bb, rows, :] + gb_ref[...]
            lf = jnp.minimum(g, 0.0) - jnp.log(1.0 + jnp.exp(-jnp.abs(g)))
            gm.append(g)
            bc.append(jnp.dot(tril, lf, precision=lax.Precision.HIGHEST, preferred_element_type=F32))
        qk = [qk_ref[bb, rows, :] for bb in range(nb)]
        vv = [zv_ref[bb, rows, :] for bb in range(nb)]
        q = [qk[bb][:, h * C_DK:(h + 1) * C_DK] for bb, h in items]
        k = [qk[bb][:, CONV_CH // 2 + h * C_DK:CONV_CH // 2 + (h + 1) * C_DK] * (C_DK ** -0.5) for bb, h in items]
        vb = [vv[bb][:, h * C_DV:(h + 1) * C_DV].astype(BF16) for bb, h in items]
        qb = [x.astype(BF16) for x in q]
        c0 = [c_ref[bb, h] for bb, h in items]
        n0 = [n_ref[bb, h:h + 1, :] for bb, h in items]
        m0 = [m_ref[bb, h][:, 0:1] for bb, h in items]
        s_raw = [lax.dot_general(qb[t], k[t].astype(BF16), nt, preferred_element_type=F32) for t in range(len(items))]
        qc = [jnp.dot(qb[t], c0[t].astype(BF16), preferred_element_type=F32) for t in range(len(items))]
        gmt = [lax.dot_general(gate_sel, x, nt, precision=lax.Precision.HIGHEST, preferred_element_type=F32)
               for x in gm]
        bct = [lax.dot_general(gate_sel, x, nt, precision=lax.Precision.HIGHEST, preferred_element_type=F32)
               for x in bc]
        m, w_i, s = [], [], []
        for t, (bb, h) in enumerate(items):
            b_col = bc[bb][:, MISC_CF + h:MISC_CF + h + 1]
            b_row = bct[bb][C_HEADS + h:C_HEADS + h + 1, :]
            ig_row = gmt[bb][h:h + 1, :]
            d = jnp.where(causal, b_col - b_row + ig_row, NEG)
            inter = b_col + m0[t]
            mm = jnp.maximum(inter, jnp.max(d, axis=1, keepdims=True))
            m.append(mm)
            w_i.append(jnp.exp(inter - mm))
            s.append(s_raw[t] * jnp.exp(d - mm))
        sv = [jnp.dot(s[t].astype(BF16), vb[t], preferred_element_type=F32) for t in range(len(items))]
        ks, w_c, m_t = [], [], []
        for t, (bb, h) in enumerate(items):
            b_col = bc[bb][:, MISC_CF + h:MISC_CF + h + 1]
            ig_col = gm[bb][:, MISC_CI + h:MISC_CI + h + 1]
            mt = m[t][n - 1:n, :]
            b_last = b_col[n - 1:n, :]
            ks.append(k[t] * jnp.exp(b_last - b_col + ig_col - mt))
            w_c.append(jnp.exp(b_last + m0[t] - mt))
            m_t.append(mt)
        kst = [lax.dot_general(eye_k, x.astype(BF16), nt, preferred_element_type=F32).astype(BF16) for x in ks]
        upd = [jnp.dot(kst[t], vb[t], preferred_element_type=F32) for t in range(len(items))]
        zo = [zo_ref[bb, rows, :] for bb in range(nb)]
        for t, (bb, h) in enumerate(items):
            num = sv[t] + w_i[t] * qc[t]
            den = jnp.sum(s[t], axis=1, keepdims=True) + w_i[t] * jnp.sum(q[t] * n0[t], axis=1, keepdims=True)
            hh = num / jnp.maximum(jnp.abs(den), jnp.exp(-m[t]))
            c_ref[bb, h] = w_c[t] * c0[t] + upd[t]
            n_ref[bb, h:h + 1, :] = w_c[t] * n0[t] + jnp.sum(ks[t], axis=0, keepdims=True)
            m_ref[bb, h] = jnp.broadcast_to(m_t[t], (1, LANES))
            yn = hh * lax.rsqrt(jnp.mean(hh * hh, axis=-1, keepdims=True) + EPS) * gn_ref[...]
            hc_ref[bb, rows, h * C_DV:(h + 1) * C_DV] = \
                (yn * _sigmoid(zo[bb][:, h * C_DV:(h + 1) * C_DV])).astype(BF16)
        return carry

    lax.fori_loop(0, tb // n, chunk, 0)

    @pl.when(ci == pl.num_programs(1) - 1)
    def _():
        for bb in range(nb):
            c1_ref[bb] = c_ref[bb]
            n1_ref[bb] = n_ref[bb]
            m1_ref[bb] = jnp.concatenate([m_ref[bb, h][:, 0:1] for h in range(C_HEADS)], axis=1)


def _mlstm(z3, conv_w, conv_b, ig_b, fg_b, gn, c0, n0, m0, cv0):
    bsz, t, _ = z3.shape
    tb = _row_tile(t, 256)
    assert tb % CHUNK == 0
    width = 512
    gate_bias = jnp.zeros((1, LANES), F32)
    gate_bias = gate_bias.at[0, MISC_CI:MISC_CI + C_HEADS].set(ig_b.astype(F32))
    gate_bias = gate_bias.at[0, MISC_CF:MISC_CF + C_HEADS].set(fg_b.astype(F32))

    nb = MLSTM_BATCH if bsz % MLSTM_BATCH == 0 else 1

    def col(off, w=width):
        return pl.BlockSpec((nb, tb, w), lambda b, i: (b, i, off // w))

    def full(shape):
        nd = len(shape)
        return pl.BlockSpec(shape, lambda b, i: (0,) * nd)

    def per_b(shape):
        nd = len(shape)
        return pl.BlockSpec((nb,) + shape, lambda b, i: (b,) + (0,) * nd)

    return pl.pallas_call(
        functools.partial(_mlstm_kernel, tb=tb, nb=nb),
        grid=(bsz // nb, t // tb),
        in_specs=[col(COL_CQK), col(COL_CV), col(COL_CO), col(COL_MISC, LANES),
                  full((CONV_W, CONV_CH)), full((1, CONV_CH)), full((1, LANES)), full((1, C_DV)),
                  per_b((C_HEADS, C_DK, C_DV)), per_b((C_HEADS, C_DK)), per_b((1, C_HEADS)),
                  per_b((CONV_W - 1, CONV_CH))],
        out_specs=[pl.BlockSpec((nb, tb, width), lambda b, i: (b, i, 0)),
                   per_b((C_HEADS, C_DK, C_DV)), per_b((C_HEADS, C_DK)), per_b((1, C_HEADS))],
        out_shape=[jax.ShapeDtypeStruct((bsz, t, width), BF16),
                   jax.ShapeDtypeStruct((bsz, C_HEADS, C_DK, C_DV), F32),
                   jax.ShapeDtypeStruct((bsz, C_HEADS, C_DK), F32),
                   jax.ShapeDtypeStruct((bsz, 1, C_HEADS), F32)],
        scratch_shapes=[pltpu.VMEM((nb, CONV_PAD + tb, CONV_CH), F32),
                        pltpu.VMEM((nb, tb, CONV_CH), F32),
                        pltpu.VMEM((nb, C_HEADS, C_DK, C_DV), F32),
                        pltpu.VMEM((nb, C_HEADS, C_DK), F32),
                        pltpu.VMEM((nb, C_HEADS, 1, LANES), F32)],
        compiler_params=_cparams("parallel", "arbitrary"),
        name="mlstm",
    )(z3, z3, z3, z3, conv_w, conv_b.reshape(1, CONV_CH), gate_bias, gn.reshape(1, C_DV),
      c0, n0, m0.reshape(bsz, 1, C_HEADS), cv0)


def _out_proj_kernel(oa_ref, ob_ref, hc_ref, x_ref, w_ref, o_ref):
    wa = w_ref[0:1024, :]
    wb = w_ref[1024:1536, :]
    wc = w_ref[1536:2048, :]
    acc = jnp.dot(oa_ref[...], wa, preferred_element_type=F32)
    acc = acc + jnp.dot(ob_ref[...], wb, preferred_element_type=F32)
    acc = acc + jnp.dot(hc_ref[...], wc, preferred_element_type=F32)
    o_ref[...] = x_ref[...] + acc


def _out_proj(oa, ob, hc, x, w):
    n, d = x.shape
    tm = _row_tile(n, 512)
    tn = d
    return pl.pallas_call(
        _out_proj_kernel,
        grid=(n // tm, d // tn),
        in_specs=[pl.BlockSpec((tm, oa.shape[1]), lambda i, j: (i, 0)),
                  pl.BlockSpec((tm, ob.shape[1]), lambda i, j: (i, 0)),
                  pl.BlockSpec((tm, hc.shape[1]), lambda i, j: (i, 0)),
                  pl.BlockSpec((tm, tn), lambda i, j: (i, j)),
                  pl.BlockSpec((w.shape[0], tn), lambda i, j: (0, j))],
        out_specs=pl.BlockSpec((tm, tn), lambda i, j: (i, j)),
        out_shape=jax.ShapeDtypeStruct((n, d), F32),
        compiler_params=_cparams("parallel", "arbitrary"),
        name="out_proj",
    )(oa, ob, hc, x, w)


def _ffn_accumulate(hn_ref, wg_ref, wu_ref, wd_ref, o_ref):
    hn = hn_ref[...]
    a = jnp.dot(hn, wg_ref[0], preferred_element_type=F32)
    b = jnp.dot(hn, wu_ref[0], preferred_element_type=F32)
    hmid = (_silu(a) * b).astype(BF16)
    o_ref[...] += jnp.dot(hmid, wd_ref[0], preferred_element_type=F32)


def _ffn_norm(x, g_ref, hn_ref):
    y = x * lax.rsqrt(jnp.mean(x * x, axis=-1, keepdims=True) + EPS) * g_ref[...]
    hn_ref[...] = y.astype(BF16)


def _ffn_kernel(te_ref, nu_ref, x_ref, g_ref, wg_ref, wu_ref, wd_ref, o_ref, hn_ref):
    @pl.when(pl.program_id(1) == 0)
    def _():
        x = x_ref[...]
        _ffn_norm(x, g_ref, hn_ref)
        o_ref[...] = x

    _ffn_accumulate(hn_ref, wg_ref, wu_ref, wd_ref, o_ref)


def _moe_ffn_kernel(te_ref, nu_ref, idx_ref, idx_next_ref, x_hbm, g_ref, wg_ref, wu_ref, wd_ref, o_ref,
                    xg_ref, hn_ref, sem, *, n_tiles, rows_per_step):
    i = pl.program_id(0)
    f = pl.program_id(1)
    tm = o_ref.shape[0]
    n_used = nu_ref[0]

    def row_copy(ids_ref, slot, r):
        return pltpu.make_async_copy(x_hbm.at[pl.ds(ids_ref[0, 0, r], 1), :],
                                     xg_ref.at[slot, pl.ds(r, 1), :], sem.at[slot])

    def start_rows(ids_ref, slot, lo, hi):
        def body(r, c):
            row_copy(ids_ref, slot, r).start()
            return c
        lax.fori_loop(lo, hi, body, 0)

    @pl.when((i == 0) & (f == 0) & (n_used > 0))
    def _():
        start_rows(idx_ref, 0, 0, tm)

    @pl.when((i + 1 < n_tiles) & (i + 1 < n_used))
    def _():
        lo = f * rows_per_step
        start_rows(idx_next_ref, (i + 1) % 2, lo, jnp.minimum(lo + rows_per_step, tm))

    @pl.when(i < n_used)
    def _():
        @pl.when(f == 0)
        def _():
            slot = i % 2

            def body(r, c):
                row_copy(idx_ref, slot, r).wait()
                return c
            lax.fori_loop(0, tm, body, 0)
            _ffn_norm(xg_ref[slot], g_ref, hn_ref)
            o_ref[...] = jnp.zeros_like(o_ref)

        _ffn_accumulate(hn_ref, wg_ref, wu_ref, wd_ref, o_ref)

    @pl.when((i >= n_used) & (f == 0))
    def _():
        o_ref[...] = jnp.zeros_like(o_ref)


def _ffn_specs(d, tf):
    def w_in_map(i, f, te, nu):
        return (te[i], 0, jnp.where(i < nu[0], f, 0))

    def w_out_map(i, f, te, nu):
        return (te[i], jnp.where(i < nu[0], f, 0), 0)

    return [pl.BlockSpec((1, d), lambda i, f, te, nu: (0, 0)),
            pl.BlockSpec((1, d, tf), w_in_map),
            pl.BlockSpec((1, d, tf), w_in_map),
            pl.BlockSpec((1, tf, d), w_out_map)]


def _ffn_dense(x, g, wg, wu, wd):
    n, d = x.shape
    ff = wg.shape[2]
    tm = _row_tile(n, 512)
    assert ff % FFN_COLS == 0
    grid_spec = pltpu.PrefetchScalarGridSpec(
        num_scalar_prefetch=2,
        grid=(n // tm, ff // FFN_COLS),
        in_specs=[pl.BlockSpec((tm, d), lambda i, f, te, nu: (i, 0))] + _ffn_specs(d, FFN_COLS),
        out_specs=pl.BlockSpec((tm, d), lambda i, f, te, nu: (i, 0)),
        scratch_shapes=[pltpu.VMEM((tm, d), BF16)],
    )
    return pl.pallas_call(
        _ffn_kernel,
        grid_spec=grid_spec,
        out_shape=jax.ShapeDtypeStruct((n, d), F32),
        compiler_params=_cparams("parallel", "arbitrary"),
        name="swiglu_ffn",
    )(jnp.zeros((n // tm,), I32), jnp.full((1,), n // tm, I32), x, g.reshape(1, d), wg, wu, wd)


def _ffn_moe(x, src_tok, g, wg, wu, wd, tile_expert, n_used, tm):
    d = x.shape[1]
    p = src_tok.shape[0]
    ff = wg.shape[2]
    n_tiles = p // tm
    nf = ff // FFN_COLS
    assert ff % FFN_COLS == 0 and p % tm == 0
    ids = src_tok.reshape(n_tiles, 1, tm)
    grid_spec = pltpu.PrefetchScalarGridSpec(
        num_scalar_prefetch=2,
        grid=(n_tiles, nf),
        in_specs=[pl.BlockSpec((1, 1, tm), lambda i, f, te, nu: (i, 0, 0), memory_space=pltpu.SMEM),
                  pl.BlockSpec((1, 1, tm), lambda i, f, te, nu: (jnp.minimum(i + 1, n_tiles - 1), 0, 0),
                               memory_space=pltpu.SMEM),
                  pl.BlockSpec(memory_space=pl.ANY)] + _ffn_specs(d, FFN_COLS),
        out_specs=pl.BlockSpec((tm, d), lambda i, f, te, nu: (i, 0)),
        scratch_shapes=[pltpu.VMEM((2, tm, d), F32), pltpu.VMEM((tm, d), BF16), pltpu.SemaphoreType.DMA((2,))],
    )
    return pl.pallas_call(
        functools.partial(_moe_ffn_kernel, n_tiles=n_tiles, rows_per_step=-(-tm // nf)),
        grid_spec=grid_spec,
        out_shape=jax.ShapeDtypeStruct((p, d), F32),
        compiler_params=_cparams("arbitrary", "arbitrary"),
        name="moe_ffn",
    )(tile_expert, n_used, ids, ids, x, g.reshape(1, d), wg, wu, wd)


def _router_kernel(x_ref, g_ref, w_ref, b_ref, e_ref, p_ref):
    x = x_ref[...]
    h = x * lax.rsqrt(jnp.mean(x * x, axis=-1, keepdims=True) + EPS) * g_ref[...]
    logits = jnp.dot(h, w_ref[...], precision=lax.Precision.HIGHEST, preferred_element_type=F32) + b_ref[...]
    lane = lax.broadcasted_iota(I32, logits.shape, 1)
    logits = jnp.where(lane < N_EXPERTS, logits, -jnp.inf)
    m1 = jnp.max(logits, axis=1, keepdims=True)
    i1 = jnp.min(jnp.where(logits == m1, lane, LANES), axis=1, keepdims=True)
    rest = jnp.where(lane == i1, -jnp.inf, logits)
    m2 = jnp.max(rest, axis=1, keepdims=True)
    i2 = jnp.min(jnp.where(rest == m2, lane, LANES), axis=1, keepdims=True)
    e2 = jnp.exp(m2 - m1)
    den = 1.0 + e2
    e_ref[...] = jnp.where(lane == 0, i1, jnp.where(lane == 1, i2, 0))
    p_ref[...] = jnp.where(lane == 0, 1.0 / den, jnp.where(lane == 1, e2 / den, 0.0))


def _router(x, g, w_router, b_router):
    n, d = x.shape
    tm = _row_tile(n, 512)
    w = jnp.zeros((d, LANES), F32).at[:, :N_EXPERTS].set(w_router.astype(F32))
    b = jnp.zeros((1, LANES), F32).at[0, :N_EXPERTS].set(b_router.astype(F32))
    return pl.pallas_call(
        _router_kernel,
        grid=(n // tm,),
        in_specs=[pl.BlockSpec((tm, d), lambda i: (i, 0)),
                  pl.BlockSpec((1, d), lambda i: (0, 0)),
                  pl.BlockSpec((d, LANES), lambda i: (0, 0)),
                  pl.BlockSpec((1, LANES), lambda i: (0, 0))],
        out_specs=[pl.BlockSpec((tm, LANES), lambda i: (i, 0)),
                   pl.BlockSpec((tm, LANES), lambda i: (i, 0))],
        out_shape=[jax.ShapeDtypeStruct((n, LANES), I32),
                   jax.ShapeDtypeStruct((n, LANES), F32)],
        compiler_params=_cparams("parallel"),
        name="moe_router",
    )(x, g.reshape(1, d), w, b)


def _combine_kernel(i1_ref, i2_ref, x_ref, p_ref, yb_ref, o_ref, r1_ref, r2_ref, sem):
    rows = o_ref.shape[0]

    def row_copy(idx_ref, dst_ref, r):
        return pltpu.make_async_copy(yb_ref.at[pl.ds(idx_ref[0, 0, r], 1), :], dst_ref.at[pl.ds(r, 1), :], sem)

    def start(r, c):
        row_copy(i1_ref, r1_ref, r).start()
        row_copy(i2_ref, r2_ref, r).start()
        return c

    def wait(r, c):
        row_copy(i1_ref, r1_ref, r).wait()
        row_copy(i2_ref, r2_ref, r).wait()
        return c

    lax.fori_loop(0, rows, start, 0)
    lax.fori_loop(0, rows, wait, 0)
    p = p_ref[...]
    o_ref[...] = x_ref[...] + (r1_ref[...] * p[:, 0:1] + r2_ref[...] * p[:, 1:2])


def _combine(x, yb, i1, i2, gates):
    n, d = x.shape
    rows = _row_tile(n, GATHER_ROWS)
    row = pl.BlockSpec((rows, d), lambda i: (i, 0))
    idx = pl.BlockSpec((1, 1, rows), lambda i: (i, 0, 0), memory_space=pltpu.SMEM)
    return pl.pallas_call(
        _combine_kernel,
        grid=(n // rows,),
        in_specs=[idx, idx, row, pl.BlockSpec((rows, LANES), lambda i: (i, 0)),
                  pl.BlockSpec(memory_space=pl.ANY)],
        out_specs=row,
        out_shape=jax.ShapeDtypeStruct((n, d), F32),
        scratch_shapes=[pltpu.VMEM((rows, d), F32), pltpu.VMEM((rows, d), F32), pltpu.SemaphoreType.DMA(())],
        compiler_params=_cparams("arbitrary"),
        name="moe_combine",
    )(i1.reshape(n // rows, 1, rows), i2.reshape(n // rows, 1, rows), x, gates, yb)


def _moe(x, g, w_router, b_router, wg, wu, wd):
    n, d = x.shape
    e_out, gates = _router(x, g, w_router, b_router)
    flat_e = e_out[:, :TOP_K].reshape(n * TOP_K)
    nk = n * TOP_K
    tm = 512 if nk >= 8192 else 128
    onehot = (flat_e[:, None] == jnp.arange(N_EXPERTS, dtype=I32)[None, :]).astype(I32)
    csum = jnp.cumsum(onehot, axis=0)
    counts = csum[-1]
    rank = jnp.sum((csum - 1) * onehot, axis=1)
    padded = (counts + tm - 1) // tm * tm
    pad_end = jnp.cumsum(padded)
    pad_start = pad_end - padded
    dest = (pad_start[flat_e] + rank).astype(I32)
    n_tiles = (nk + N_EXPERTS * (tm - 1) + tm - 1) // tm
    p = n_tiles * tm
    src_tok = jnp.zeros((p,), I32).at[dest].set(jnp.arange(nk, dtype=I32) // TOP_K)
    tile_expert = jnp.minimum(jnp.searchsorted(pad_end, jnp.arange(n_tiles, dtype=I32) * tm, side='right'),
                              N_EXPERTS - 1).astype(I32)
    n_used = (pad_end[-1] // tm).astype(I32).reshape(1)

    yb = _ffn_moe(x, src_tok, g, wg, wu, wd, tile_expert, n_used, tm)
    dest2 = dest.reshape(n, TOP_K)
    return _combine(x, yb, dest2[:, 0], dest2[:, 1], gates)


def _reorder_w_in(w):
    offs, o = {}, 0
    for name, width in (("a_q", 1024), ("a_k", 256), ("a_v", 256), ("i_q", 512), ("i_w", 8), ("i_k", 64),
                        ("b_q", 512), ("b_f", 512), ("b_i", 512), ("b_g", 512), ("c_qk", 512), ("c_v", 512),
                        ("c_i", 4), ("c_f", 4), ("c_o", 512)):
        offs[name] = (o, width)
        o += width

    def c(name):
        s, width = offs[name]
        return w[:, s:s + width]

    pad = jnp.zeros((w.shape[0], LANES - (IDX_DIM + IDX_HEADS + 2 * C_HEADS)), w.dtype)
    cols = [c("a_q"), c("a_k"), c("a_v"), c("i_q"), c("b_q"), c("b_f"), c("b_i"), c("b_g"),
            c("c_qk"), c("c_v"), c("c_o"), c("i_k"), c("i_w"), c("c_i"), c("c_f"), pad]
    return jnp.concatenate(cols, axis=1).astype(BF16)


def _trunk(x, pos0, W, lbs, past):
    bsz, t, d = x.shape
    n = bsz * t
    pos = pos0 + jnp.arange(t, dtype=I32)
    xf = x.reshape(n, d)
    states = []
    depth = W['w_in'].shape[0]
    for l in range(depth):
        z = _norm_matmul(xf, W['attn_norm'][l], W['w_in_r'][l], tn=1152)
        z3 = z.reshape(bsz, t, Z_WIDTH)
        q, k, qi, ki = _prep(z3, W['q_norm'][l], W['k_norm'][l], pos)
        v = z3[:, :, COL_AV:COL_AV + 256]
        if past is None:
            n_keys = t
            assert t % KEY_BLOCK == 0
            o_a = _sparse_attention(q, qi, z3, k, z3, COL_AV // 256, ki, n_keys, 0)
            s0 = jnp.zeros((bsz, B_HEADS, B_DK, B_DV), F32)
            c0 = jnp.zeros((bsz, C_HEADS, C_DK, C_DV), F32)
            n0 = jnp.zeros((bsz, C_HEADS, C_DK), F32)
            m0 = jnp.zeros((bsz, C_HEADS), F32)
            cv0 = jnp.zeros((bsz, CONV_W - 1, CONV_CH), F32)
        else:
            pk = past['k'][l].reshape(bsz, -1, 256)
            pv = past['v'][l].reshape(bsz, -1, 256)
            n_keys = pk.shape[1] + t
            lp = (n_keys + KEY_BLOCK - 1) // KEY_BLOCK * KEY_BLOCK
            padk = jnp.zeros((bsz, lp - n_keys, 256), F32)
            k_all = jnp.concatenate([pk, k, padk], axis=1)
            v_all = jnp.concatenate([pv, v, padk], axis=1)
            ki_all = jnp.concatenate([past['ki'][l], ki, padk[:, :, :IDX_DIM]], axis=1)
            o_a = _sparse_attention(q, qi, z3, k_all, v_all, 0, ki_all, n_keys, pk.shape[1])
            s0, c0, n0, m0, cv0 = past['hgrn'][l], past['C'][l], past['n'][l], past['m'][l], past['conv'][l]
        ob, s1 = _hgrn(z3, lbs[l], W['hgrn_norm'][l], s0)
        hc, c1, n1, m1 = _mlstm(z3, W['conv_w'][l], W['conv_b'][l], W['ig_b'][l], W['fg_b'][l],
                                W['mlstm_norm'][l], c0, n0, m0, cv0)
        cqk = z3[:, :, COL_CQK:COL_CQK + CONV_CH]
        conv1 = jnp.concatenate([cv0, cqk], axis=1)[:, -(CONV_W - 1):]
        xf = _out_proj(o_a.reshape(n, -1), ob.reshape(n, -1), hc.reshape(n, -1), xf, W['w_out_b'][l])
        jj = l // 2
        if l % 2 == 0:
            xf = _ffn_dense(xf, W['ffn_norm'][l], W['dense_wg_b'][jj][None], W['dense_wu_b'][jj][None],
                            W['dense_wd_b'][jj][None])
        else:
            xf = _moe(xf, W['ffn_norm'][l], W['moe_router'][jj], W['moe_router_b'][jj],
                      W['moe_wg_b'][jj], W['moe_wu_b'][jj], W['moe_wd_b'][jj])
        states.append(dict(k=k.reshape(bsz, t, A_KV_HEADS, HEAD_DIM), v=v.reshape(bsz, t, A_KV_HEADS, HEAD_DIM),
                           ki=ki, hgrn=s1, C=c1, n=n1, m=m1.reshape(bsz, C_HEADS), conv=conv1))
    stacked = {name: jnp.stack([s[name] for s in states]) for name in states[0]}
    return xf.reshape(bsz, t, d), stacked


def kernel(x_prompt, x_sample, cache_k, cache_v, cache_kidx, state_hgrn, state_mlstm_C, state_mlstm_n,
           state_mlstm_m, state_mlstm_conv, attn_norm, w_in, w_out, q_norm, k_norm, hgrn_lb_logits,
           hgrn_norm, conv_w, conv_b, ig_b, fg_b, mlstm_norm, ffn_norm, dense_wg, dense_wu, dense_wd,
           moe_router, moe_router_b, moe_wg, moe_wu, moe_wd):
    depth = w_in.shape[0]
    W = dict(attn_norm=attn_norm, w_in=w_in, q_norm=q_norm, k_norm=k_norm, hgrn_norm=hgrn_norm,
             conv_w=conv_w, conv_b=conv_b, ig_b=ig_b, fg_b=fg_b, mlstm_norm=mlstm_norm, ffn_norm=ffn_norm,
             moe_router=moe_router, moe_router_b=moe_router_b,
             w_in_r=jnp.stack([_reorder_w_in(w_in[l]) for l in range(depth)]),
             w_out_b=w_out.astype(BF16),
             dense_wg_b=dense_wg.astype(BF16), dense_wu_b=dense_wu.astype(BF16), dense_wd_b=dense_wd.astype(BF16),
             moe_wg_b=moe_wg.astype(BF16), moe_wu_b=moe_wu.astype(BF16), moe_wd_b=moe_wd.astype(BF16))
    p_lb = jax.nn.softmax(hgrn_lb_logits.astype(F32), axis=0)
    lbs = jnp.cumsum(p_lb, axis=0) - p_lb[0]
    y_prompt, sp = _trunk(x_prompt, 0, W, lbs, None)
    past = dict(k=cache_k, v=cache_v, ki=cache_kidx, hgrn=state_hgrn, C=state_mlstm_C, n=state_mlstm_n,
                m=state_mlstm_m, conv=state_mlstm_conv)
    y_sample, ss = _trunk(x_sample, cache_k.shape[2], W, lbs, past)
    return (y_prompt, y_sample,
            sp['k'], sp['v'], sp['ki'], sp['hgrn'], sp['C'], sp['n'], sp['m'], sp['conv'],
            ss['k'], ss['v'], ss['ki'], ss['hgrn'], ss['C'], ss['n'], ss['m'], ss['conv'])
```

```python
import functools
import math

import numpy as np
import jax
import jax.numpy as jnp
from jax import lax
from jax.experimental import pallas as pl
from jax.experimental.pallas import tpu as pltpu

F32 = jnp.float32
BF16 = jnp.bfloat16
I32 = jnp.int32

CHUNK = 64
EPS = 1e-6
NEG = -1e30
HEAD_DIM = 128
A_HEADS = 8
A_KV_HEADS = 2
GQA = A_HEADS // A_KV_HEADS
IDX_HEADS = 8
IDX_DIM = 64
TOPK_MAX = 256
ROPE_THETA = 500000.0
ROT_FRAC = 4
ATTN_SCALE = HEAD_DIM ** -0.5
IDX_SCALE = (IDX_HEADS * IDX_DIM) ** -0.5
B_HEADS = 4
B_DK = 128
B_DV = 128
C_HEADS = 4
C_DK = 64
C_DV = 128
CONV_W = 4
CONV_CH = 2 * C_HEADS * C_DK
N_EXPERTS = 8
TOP_K = 2

COL_AQ, COL_AK, COL_AV, COL_IQ = 0, 1024, 1280, 1536
COL_BQ, COL_BF, COL_BI, COL_BG = 2048, 2560, 3072, 3584
COL_CQK, COL_CV, COL_CO, COL_MISC = 4096, 4608, 5120, 5632
Z_WIDTH = 5760
MISC_IK, MISC_IW, MISC_CI, MISC_CF = 0, 64, 72, 76

LANES = 128
VMEM_LIMIT_BYTES = 56 * 1024 * 1024
INT_MIN = -2 ** 31

HGRN_SUB = 8
KEY_BLOCK = 512
ATTN_ROWS = 256
GATHER_ROWS = 256
FFN_COLS = 512
DMA_UNROLL = 8
MLSTM_BATCH = 2


def _cparams(*sem):
    return pltpu.CompilerParams(dimension_semantics=sem, vmem_limit_bytes=VMEM_LIMIT_BYTES)


def _sigmoid(x):
    return 1.0 / (1.0 + jnp.exp(-x))


def _silu(x):
    return x * _sigmoid(x)


def _row_tile(n, pref):
    t = min(n, pref)
    while n % t:
        t //= 2
    return t


def _norm_matmul_kernel(x_ref, g_ref, w_ref, o_ref, xn_ref):
    @pl.when(pl.program_id(1) == 0)
    def _():
        x = x_ref[...]
        y = x * lax.rsqrt(jnp.mean(x * x, axis=-1, keepdims=True) + EPS) * g_ref[...]
        xn_ref[...] = y.astype(BF16)

    o_ref[...] = jnp.dot(xn_ref[...], w_ref[...], preferred_element_type=F32)


def _norm_matmul(x, g, w, tn):
    n, d = x.shape
    nout = w.shape[1]
    tm = _row_tile(n, 1024)
    return pl.pallas_call(
        _norm_matmul_kernel,
        grid=(n // tm, nout // tn),
        in_specs=[pl.BlockSpec((tm, d), lambda i, j: (i, 0)),
                  pl.BlockSpec((1, d), lambda i, j: (0, 0)),
                  pl.BlockSpec((d, tn), lambda i, j: (0, j))],
        out_specs=pl.BlockSpec((tm, tn), lambda i, j: (i, j)),
        out_shape=jax.ShapeDtypeStruct((n, nout), F32),
        scratch_shapes=[pltpu.VMEM((tm, d), BF16)],
        compiler_params=_cparams("parallel", "arbitrary"),
        name="norm_in_proj",
    )(x, g.reshape(1, d), w)


def _rope(x, c, sa, sb, half):
    w = x.shape[-1]
    return x * c + pltpu.roll(x, w - half, 1) * sa + pltpu.roll(x, half, 1) * sb


def _prep_kernel(zq_ref, zk_ref, ziq_ref, zm_ref, qn_ref, kn_ref, t128_ref, t64_ref,
                 q_ref, k_ref, qi_ref, ki_ref):
    c1, sa1, sb1 = t128_ref[0], t128_ref[1], t128_ref[2]
    c2, sa2, sb2 = t64_ref[0], t64_ref[1], t64_ref[2]
    half1 = HEAD_DIM // ROT_FRAC // 2
    half2 = IDX_DIM // ROT_FRAC // 2

    def normed(x, g):
        return x * lax.rsqrt(jnp.mean(x * x, axis=-1, keepdims=True) + EPS) * g

    for h in range(A_HEADS):
        x = zq_ref[0, :, h * HEAD_DIM:(h + 1) * HEAD_DIM]
        q_ref[0, h] = _rope(normed(x, qn_ref[...]), c1, sa1, sb1, half1)
    for h in range(A_KV_HEADS):
        x = zk_ref[0, :, h * HEAD_DIM:(h + 1) * HEAD_DIM]
        k_ref[0, :, h * HEAD_DIM:(h + 1) * HEAD_DIM] = _rope(normed(x, kn_ref[...]), c1, sa1, sb1, half1)
    for j in range(IDX_HEADS * IDX_DIM // LANES):
        x = ziq_ref[0, :, j * LANES:(j + 1) * LANES]
        qi_ref[0, :, j * LANES:(j + 1) * LANES] = _rope(x, c2, sa2, sb2, half2)
    m = _rope(zm_ref[0], c2, sa2, sb2, half2)
    ki_ref[0] = m[:, MISC_IK:MISC_IK + IDX_DIM]


def _rope_tables(pos, period):
    rot = period // ROT_FRAC
    half = rot // 2
    inv = jnp.exp(-math.log(ROPE_THETA) * 2.0 * jnp.arange(half, dtype=F32) / rot)
    ang = pos.astype(F32)[:, None] * inv
    cos, sin = jnp.cos(ang), jnp.sin(ang)
    t = pos.shape[0]
    ones = jnp.ones((t, period - rot), F32)
    zeros_h = jnp.zeros((t, half), F32)
    zeros_r = jnp.zeros((t, period - rot), F32)
    c = jnp.concatenate([cos, cos, ones], axis=1)
    sa = jnp.concatenate([-sin, zeros_h, zeros_r], axis=1)
    sb = jnp.concatenate([zeros_h, sin, zeros_r], axis=1)
    tab = jnp.stack([c, sa, sb])
    return jnp.tile(tab, (1, 1, LANES // period))


def _prep(z3, q_norm, k_norm, pos):
    bsz, t, _ = z3.shape
    tt = _row_tile(t, 256)
    t128 = _rope_tables(pos, HEAD_DIM)
    t64 = _rope_tables(pos, IDX_DIM)

    def col(width, off):
        return pl.BlockSpec((1, tt, width), lambda b, i: (b, i, off // width))

    return pl.pallas_call(
        _prep_kernel,
        grid=(bsz, t // tt),
        in_specs=[col(1024, COL_AQ), col(256, COL_AK), col(512, COL_IQ), col(128, COL_MISC),
                  pl.BlockSpec((1, HEAD_DIM), lambda b, i: (0, 0)),
                  pl.BlockSpec((1, HEAD_DIM), lambda b, i: (0, 0)),
                  pl.BlockSpec((3, tt, LANES), lambda b, i: (0, i, 0)),
                  pl.BlockSpec((3, tt, LANES), lambda b, i: (0, i, 0))],
        out_specs=[pl.BlockSpec((1, A_HEADS, tt, HEAD_DIM), lambda b, i: (b, 0, i, 0)),
                   pl.BlockSpec((1, tt, A_KV_HEADS * HEAD_DIM), lambda b, i: (b, i, 0)),
                   pl.BlockSpec((1, tt, IDX_HEADS * IDX_DIM), lambda b, i: (b, i, 0)),
                   pl.BlockSpec((1, tt, IDX_DIM), lambda b, i: (b, i, 0))],
        out_shape=[jax.ShapeDtypeStruct((bsz, A_HEADS, t, HEAD_DIM), F32),
                   jax.ShapeDtypeStruct((bsz, t, A_KV_HEADS * HEAD_DIM), F32),
                   jax.ShapeDtypeStruct((bsz, t, IDX_HEADS * IDX_DIM), F32),
                   jax.ShapeDtypeStruct((bsz, t, IDX_DIM), F32)],
        compiler_params=_cparams("parallel", "parallel"),
        name="qk_prep",
    )(z3, z3, z3, z3, q_norm.reshape(1, HEAD_DIM), k_norm.reshape(1, HEAD_DIM), t128, t64)


def _f2key(x):
    bits = lax.bitcast_convert_type(x, I32)
    return jnp.where(bits < 0, bits ^ 0x7FFFFFFF, bits)


def _attn_kernel(q_ref, qi_ref, zm_ref, k_ref, v_ref, ki_ref, o_ref,
                 key_ref, bias_ref, qs_ref, kb_ref, vb_ref, kib_ref, acc_ref, m_ref, l_ref, *,
                 tq, n_keys, topk, q_pos0, kb_size, lp):
    j = pl.program_id(1)
    q_first = q_pos0 + j * tq
    lim = jnp.minimum(((q_first + tq - 1) // CHUNK + 1) * CHUNK, n_keys)
    n_kb = (lim + kb_size - 1) // kb_size

    @pl.when(j == 0)
    def _():
        def cast_body(kb, c):
            sl = pl.ds(pl.multiple_of(kb * kb_size, kb_size), kb_size)
            kb_ref[sl, :] = k_ref[0, sl, :].astype(BF16)
            vb_ref[sl, :] = v_ref[0, sl, :].astype(BF16)
            kib_ref[sl, :] = ki_ref[0, sl, :].astype(BF16)
            return c
        lax.fori_loop(0, lp // kb_size, cast_body, 0)

    for h in range(A_HEADS):
        qs_ref[h // GQA, (h % GQA) * tq:(h % GQA + 1) * tq, :] = \
            (q_ref[0, h] * (ATTN_SCALE * math.log2(math.e))).astype(BF16)

    row = lax.broadcasted_iota(I32, (tq, kb_size), 0)
    lane = lax.broadcasted_iota(I32, (tq, kb_size), 1)
    q_chunk = lax.shift_right_logical(q_first + row, 6)

    qi = qi_ref[0].astype(BF16)
    wi = zm_ref[0][:, MISC_IW:MISC_IW + IDX_HEADS] * IDX_SCALE

    def score_body(kb, carry):
        k0 = pl.multiple_of(kb * kb_size, kb_size)
        ki_blk = kib_ref[pl.ds(k0, kb_size), :]
        s = jnp.zeros((tq, kb_size), F32)
        for h in range(IDX_HEADS):
            r = lax.dot_general(qi[:, h * IDX_DIM:(h + 1) * IDX_DIM], ki_blk,
                                (((1,), (1,)), ((), ())), preferred_element_type=F32)
            s = s + jnp.maximum(r, 0.0) * wi[:, h:h + 1]
        kidx = k0 + lane
        admiss = (lax.shift_right_logical(kidx, 6) <= q_chunk) & (kidx < n_keys)
        key_ref[kb] = _f2key(jnp.where(admiss, s, NEG))
        return carry

    lax.fori_loop(0, n_kb, score_body, 0)

    def count(pred_fn):
        def body(kb, acc):
            m = pred_fn(key_ref[kb], kb).astype(I32)
            for c in range(kb_size // LANES):
                acc = acc + m[:, c * LANES:(c + 1) * LANES]
            return acc
        acc = lax.fori_loop(0, n_kb, body, jnp.zeros((tq, LANES), I32))
        return jnp.sum(acc, axis=1, keepdims=True)

    q_chunk1 = lax.shift_right_logical(q_first + lax.broadcasted_iota(I32, (tq, 1), 0), 6)
    n_adm = jnp.minimum((q_chunk1 + 1) * CHUNK, n_keys)
    neg_key = int(np.float32(NEG).view(np.int32)) ^ 0x7FFFFFFF
    few = n_adm <= topk

    def search_cond(st):
        step, prefix, thr, c_thr, done = st
        return (step < 32) & (jnp.min(done) == 0)

    def search_body(st):
        step, prefix, thr, c_thr, done = st
        cand = prefix ^ lax.shift_left(jnp.int32(1), 31 - step)
        c = count(lambda key, kb: key >= cand)
        ge = c >= topk
        hit = (c == topk) & (done == 0)
        return (step + 1, jnp.where(ge, cand, prefix), jnp.where(hit, cand, thr),
                jnp.where(hit, c, c_thr), jnp.where(hit, 1, done))

    init = (jnp.int32(0), jnp.full((tq, 1), INT_MIN, I32), jnp.full((tq, 1), neg_key, I32),
            jnp.full((tq, 1), topk, I32), few.astype(I32))
    step_f, prefix_f, thr, c_ge, done = lax.while_loop(search_cond, search_body, init)
    thr = jnp.where(done != 0, thr, prefix_f)
    c_ge = jnp.where(done != 0, c_ge, topk + 1)

    def tie_limit():
        need = topk - count(lambda key, kb: key > thr)

        def idx_body(i, x):
            cand = x | lax.shift_left(jnp.int32(1), 14 - i)
            c = count(lambda key, kb: (key == thr) & (kb * kb_size + lane < cand))
            return jnp.where(c < need, cand, x)
        return lax.fori_loop(0, 15, idx_body, jnp.zeros((tq, 1), I32))

    xlim = lax.cond(jnp.max(c_ge) > topk, tie_limit, lambda: jnp.full((tq, 1), 2 ** 30, I32))

    valid_key = int(np.float32(0.5 * NEG).view(np.int32)) ^ 0x7FFFFFFF

    def mask_body(kb, carry):
        key = key_ref[kb]
        sel = (key > thr) | ((key == thr) & (kb * kb_size + lane <= xlim))
        bias_ref[kb] = jnp.where(sel & (key > valid_key), 0.0, NEG)
        return carry

    lax.fori_loop(0, n_kb, mask_body, 0)

    n_col = kb_size // LANES

    def logits(g, kb):
        sl = pl.ds(pl.multiple_of(kb * kb_size, kb_size), kb_size)
        k_blk = kb_ref[sl, g * HEAD_DIM:(g + 1) * HEAD_DIM]
        s = lax.dot_general(qs_ref[g], k_blk, (((1,), (1,)), ((), ())), preferred_element_type=F32)
        return s.reshape(GQA, tq, kb_size) + bias_ref[kb][None]

    for g in range(A_KV_HEADS):
        def max_body(kb, mx, g=g):
            s = logits(g, kb)
            for c in range(n_col):
                mx = jnp.maximum(mx, s[:, :, c * LANES:(c + 1) * LANES])
            return mx

        mx = lax.fori_loop(0, n_kb, max_body, jnp.full((GQA, tq, LANES), NEG, F32))
        m_ref[...] = jnp.broadcast_to(jnp.max(mx, axis=2, keepdims=True), (GQA, tq, LANES))
        l_ref[...] = jnp.zeros(l_ref.shape, F32)
        acc_ref[...] = jnp.zeros(acc_ref.shape, F32)

        def att_body(kb, carry, g=g):
            sl = pl.ds(pl.multiple_of(kb * kb_size, kb_size), kb_size)
            v_blk = vb_ref[sl, g * HEAD_DIM:(g + 1) * HEAD_DIM]
            mb = m_ref[...]
            p = jnp.exp2(logits(g, kb) - jnp.concatenate([mb] * n_col, axis=2))
            lsum = l_ref[...]
            for c in range(n_col):
                lsum = lsum + p[:, :, c * LANES:(c + 1) * LANES]
            l_ref[...] = lsum
            pv = jnp.dot(p.reshape(GQA * tq, kb_size).astype(BF16), v_blk, preferred_element_type=F32)
            acc_ref[...] += pv.reshape(GQA, tq, HEAD_DIM)
            return carry

        lax.fori_loop(0, n_kb, att_body, 0)
        out = acc_ref[...] / jnp.sum(l_ref[...], axis=2, keepdims=True)
        for hh in range(GQA):
            h = g * GQA + hh
            o_ref[0, :, h * HEAD_DIM:(h + 1) * HEAD_DIM] = out[hh].astype(BF16)


def _sparse_attention(q, qi, z3, k_all, v_src, v_col, ki_all, n_keys, q_pos0):
    bsz, _, t, _ = q.shape
    lp = k_all.shape[1]
    tq = _row_tile(t, ATTN_ROWS)
    topk = min(TOPK_MAX, n_keys // 4)
    kb_size = KEY_BLOCK
    assert lp % kb_size == 0 and kb_size >= topk
    kern = functools.partial(_attn_kernel, tq=tq, n_keys=n_keys, topk=topk, q_pos0=q_pos0, kb_size=kb_size,
                             lp=lp)
    return pl.pallas_call(
        kern,
        grid=(bsz, t // tq),
        in_specs=[pl.BlockSpec((1, A_HEADS, tq, HEAD_DIM), lambda b, i: (b, 0, i, 0)),
                  pl.BlockSpec((1, tq, IDX_HEADS * IDX_DIM), lambda b, i: (b, i, 0)),
                  pl.BlockSpec((1, tq, LANES), lambda b, i: (b, i, COL_MISC // LANES)),
                  pl.BlockSpec((1, lp, 256), lambda b, i: (b, 0, 0)),
                  pl.BlockSpec((1, lp, 256), lambda b, i: (b, 0, v_col)),
                  pl.BlockSpec((1, lp, IDX_DIM), lambda b, i: (b, 0, 0))],
        out_specs=pl.BlockSpec((1, tq, A_HEADS * HEAD_DIM), lambda b, i: (b, i, 0)),
        out_shape=jax.ShapeDtypeStruct((bsz, t, A_HEADS * HEAD_DIM), BF16),
        scratch_shapes=[pltpu.VMEM((lp // kb_size, tq, kb_size), I32),
                        pltpu.VMEM((lp // kb_size, tq, kb_size), F32),
                        pltpu.VMEM((A_KV_HEADS, GQA * tq, HEAD_DIM), BF16),
                        pltpu.VMEM((lp, A_KV_HEADS * HEAD_DIM), BF16),
                        pltpu.VMEM((lp, A_KV_HEADS * HEAD_DIM), BF16),
                        pltpu.VMEM((lp, IDX_DIM), BF16),
                        pltpu.VMEM((GQA, tq, HEAD_DIM), F32),
                        pltpu.VMEM((GQA, tq, LANES), F32),
                        pltpu.VMEM((GQA, tq, LANES), F32)],
        compiler_params=_cparams("parallel", "arbitrary"),
        name="sparse_attention",
    )(q, qi, z3, k_all, v_src, ki_all)


def _hgrn_kernel(zq_ref, zf_ref, zi_ref, zg_ref, lb_ref, gn_ref, s0_ref, ob_ref, s1_ref, st_ref, *, tb):
    c = pl.program_id(1)
    sub = HGRN_SUB

    @pl.when(c == 0)
    def _():
        for h in range(B_HEADS):
            st_ref[h] = s0_ref[0, h].T

    n = CHUNK
    ns = n // sub
    rr = lax.broadcasted_iota(I32, (n, n), 0)
    cc = lax.broadcasted_iota(I32, (n, n), 1)
    tril = (rr >= cc).astype(F32)
    rowi = lax.broadcasted_iota(I32, (n, B_DK), 0)
    t_in = lax.broadcasted_iota(I32, (ns, sub, B_DK), 1)
    nt = (((1,), (1,)), ((), ()))

    def step(i, carry):
        r0 = pl.multiple_of(i * n, n)
        zq = zq_ref[0, pl.ds(r0, n), :]
        zf = zf_ref[0, pl.ds(r0, n), :]
        zi = zi_ref[0, pl.ds(r0, n), :]
        zg = zg_ref[0, pl.ds(r0, n), :]
        heads = range(B_HEADS)
        sls = [slice(h * B_DK, (h + 1) * B_DK) for h in heads]
        k, q, v, vb, b = [], [], [], [], []
        for h in heads:
            lb = lb_ref[:, sls[h]]
            f = lb + (1.0 - lb) * _sigmoid(zf[:, sls[h]])
            logf = jnp.log(jnp.maximum(f, 1e-30))
            k.append(1.0 - f)
            q.append(_silu(zq[:, sls[h]]))
            v.append(zi[:, sls[h]])
            vb.append(v[h].astype(BF16))
            b.append(jnp.dot(tril, logf, precision=lax.Precision.HIGHEST, preferred_element_type=F32))
        st = [st_ref[h] for h in heads]
        o_mm = [lax.dot_general((q[h] * jnp.exp(b[h])).astype(BF16), st[h].astype(BF16), nt,
                                preferred_element_type=F32) for h in heads]
        upd = []
        for h in heads:
            strips = [jnp.zeros((sub, n), F32)]
            for j in range(1, ns):
                ref = b[h][j * sub - 1:j * sub, :]
                qt = q[h][j * sub:(j + 1) * sub] * jnp.exp(b[h][j * sub:(j + 1) * sub] - ref)
                kt = k[h] * jnp.exp(jnp.where(rowi < j * sub, ref - b[h], NEG))
                strips.append(lax.dot_general(qt.astype(BF16), kt.astype(BF16), nt,
                                              preferred_element_type=F32))
            o_mm[h] = o_mm[h] + jnp.dot(jnp.concatenate(strips, axis=0).astype(BF16), vb[h],
                                        preferred_element_type=F32)
            kt = k[h] * jnp.exp(b[h][n - 1:n, :] - b[h])
            upd.append(lax.dot_general(vb[h], kt.astype(BF16), (((0,), (0,)), ((), ())),
                                       preferred_element_type=F32))
        for h in heads:
            b3 = b[h].reshape(ns, sub, B_DK)
            q3 = q[h].reshape(ns, sub, B_DK)
            k3 = k[h].reshape(ns, sub, B_DK)
            v3 = v[h].reshape(ns, sub, B_DV)
            od = jnp.zeros((ns, sub, B_DV), F32)
            for s in range(sub):
                w = jnp.exp(jnp.where(t_in >= s, b3 - b3[:, s:s + 1, :], NEG))
                a = jnp.sum(q3 * w * k3[:, s:s + 1, :], axis=-1, keepdims=True)
                od = od + a * v3[:, s:s + 1, :]
            o = o_mm[h] + od.reshape(n, B_DV)
            y = o * lax.rsqrt(jnp.mean(o * o, axis=-1, keepdims=True) + EPS) * gn_ref[...]
            ob_ref[0, pl.ds(r0, n), sls[h]] = (y * _silu(zg[:, sls[h]])).astype(BF16)
            st_ref[h] = st[h] * jnp.exp(b[h][n - 1:n, :]) + upd[h]
        return carry

    lax.fori_loop(0, tb // n, step, 0)

    @pl.when(c == pl.num_programs(1) - 1)
    def _():
        for h in range(B_HEADS):
            s1_ref[0, h] = st_ref[h].T


def _hgrn(z3, lb, gn, s0):
    bsz, t, _ = z3.shape
    tb = _row_tile(t, 256)
    width = B_HEADS * B_DK

    def col(off):
        return pl.BlockSpec((1, tb, width), lambda b, i: (b, i, off // width))

    return pl.pallas_call(
        functools.partial(_hgrn_kernel, tb=tb),
        grid=(bsz, t // tb),
        in_specs=[col(COL_BQ), col(COL_BF), col(COL_BI), col(COL_BG),
                  pl.BlockSpec((1, width), lambda b, i: (0, 0)),
                  pl.BlockSpec((1, B_DV), lambda b, i: (0, 0)),
                  pl.BlockSpec((1, B_HEADS, B_DK, B_DV), lambda b, i: (b, 0, 0, 0))],
        out_specs=[pl.BlockSpec((1, tb, width), lambda b, i: (b, i, 0)),
                   pl.BlockSpec((1, B_HEADS, B_DK, B_DV), lambda b, i: (b, 0, 0, 0))],
        out_shape=[jax.ShapeDtypeStruct((bsz, t, width), BF16),
                   jax.ShapeDtypeStruct((bsz, B_HEADS, B_DK, B_DV), F32)],
        scratch_shapes=[pltpu.VMEM((B_HEADS, B_DV, B_DK), F32)],
        compiler_params=_cparams("parallel", "arbitrary"),
        name="hgrn2",
    )(z3, z3, z3, z3, lb.reshape(1, width), gn.reshape(1, B_DV), s0)


CONV_PAD = 8


def _mlstm_kernel(zqk_ref, zv_ref, zo_ref, zm_ref, cw_ref, cb_ref, gb_ref, gn_ref,
                  c0_ref, n0_ref, m0_ref, cv0_ref,
                  hc_ref, c1_ref, n1_ref, m1_ref,
                  xe_ref, qk_ref, c_ref, n_ref, m_ref, *, tb, nb):
    ci = pl.program_id(1)
    keep = CONV_W - 1

    @pl.when(ci == 0)
    def _():
        for bb in range(nb):
            xe_ref[bb, CONV_PAD - keep:CONV_PAD, :] = cv0_ref[bb]
            c_ref[bb] = c0_ref[bb]
            n_ref[bb] = n0_ref[bb]
            m0_all = m0_ref[bb]
            for h in range(C_HEADS):
                m_ref[bb, h] = jnp.broadcast_to(m0_all[:, h:h + 1], (1, LANES))

    for bb in range(nb):
        xe_ref[bb, CONV_PAD:CONV_PAD + tb, :] = zqk_ref[bb]
        y = cb_ref[...]
        for jw in range(CONV_W):
            y = y + cw_ref[jw:jw + 1, :] * xe_ref[bb, CONV_PAD - keep + jw:CONV_PAD - keep + jw + tb, :]
        qk_ref[bb] = _silu(y)
        xe_ref[bb, CONV_PAD - keep:CONV_PAD, :] = zqk_ref[bb, tb - keep:tb, :]

    n = CHUNK
    rr = lax.broadcasted_iota(I32, (n, n), 0)
    cc = lax.broadcasted_iota(I32, (n, n), 1)
    causal = rr >= cc
    tril = causal.astype(F32)
    eye_k = (rr == cc).astype(BF16)
    gate_sel = (lax.broadcasted_iota(I32, (2 * C_HEADS, LANES), 1) ==
                lax.broadcasted_iota(I32, (2 * C_HEADS, LANES), 0) + MISC_CI).astype(F32)

    items = [(bb, h) for bb in range(nb) for h in range(C_HEADS)]
    nt = (((1,), (1,)), ((), ()))

    def chunk(i, carry):
        r0 = pl.multiple_of(i * n, n)
        rows = pl.ds(r0, n)
        gm, bc = [], []
        for bb in range(nb):
            g = zm_ref[bb, rows, :] + gb_ref[...]
            lf = jnp.minimum(g, 0.0) - jnp.log(1.0 + jnp.exp(-jnp.abs(g)))
            gm.append(g)
            bc.append(jnp.dot(tril, lf, precision=lax.Precision.HIGHEST, preferred_element_type=F32))
        qk = [qk_ref[bb, rows, :] for bb in range(nb)]
        vv = [zv_ref[bb, rows, :] for bb in range(nb)]
        q = [qk[bb][:, h * C_DK:(h + 1) * C_DK] for bb, h in items]
        k = [qk[bb][:, CONV_CH // 2 + h * C_DK:CONV_CH // 2 + (h + 1) * C_DK] * (C_DK ** -0.5) for bb, h in items]
        vb = [vv[bb][:, h * C_DV:(h + 1) * C_DV].astype(BF16) for bb, h in items]
        qb = [x.astype(BF16) for x in q]
        c0 = [c_ref[bb, h] for bb, h in items]
        n0 = [n_ref[bb, h:h + 1, :] for bb, h in items]
        m0 = [m_ref[bb, h][:, 0:1] for bb, h in items]
        s_raw = [lax.dot_general(qb[t], k[t].astype(BF16), nt, preferred_element_type=F32) for t in range(len(items))]
        qc = [jnp.dot(qb[t], c0[t].astype(BF16), preferred_element_type=F32) for t in range(len(items))]
        gmt = [lax.dot_general(gate_sel, x, nt, precision=lax.Precision.HIGHEST, preferred_element_type=F32)
               for x in gm]
        bct = [lax.dot_general(gate_sel, x, nt, precision=lax.Precision.HIGHEST, preferred_element_type=F32)
               for x in bc]
        m, w_i, s = [], [], []
        for t, (bb, h) in enumerate(items):
            b_col = bc[bb][:, MISC_CF + h:MISC_CF + h + 1]
            b_row = bct[bb][C_HEADS + h:C_HEADS + h + 1, :]
            ig_row = gmt[bb][h:h + 1, :]
            d = jnp.where(causal, b_col - b_row + ig_row, NEG)
            inter = b_col + m0[t]
            mm = jnp.maximum(inter, jnp.max(d, axis=1, keepdims=True))
            m.append(mm)
            w_i.append(jnp.exp(inter - mm))
            s.append(s_raw[t] * jnp.exp(d - mm))
        sv = [jnp.dot(s[t].astype(BF16), vb[t], preferred_element_type=F32) for t in range(len(items))]
        ks, w_c, m_t = [], [], []
        for t, (bb, h) in enumerate(items):
            b_col = bc[bb][:, MISC_CF + h:MISC_CF + h + 1]
            ig_col = gm[bb][:, MISC_CI + h:MISC_CI + h + 1]
            mt = m[t][n - 1:n, :]
            b_last = b_col[n - 1:n, :]
            ks.append(k[t] * jnp.exp(b_last - b_col + ig_col - mt))
            w_c.append(jnp.exp(b_last + m0[t] - mt))
            m_t.append(mt)
        kst = [lax.dot_general(eye_k, x.astype(BF16), nt, preferred_element_type=F32).astype(BF16) for x in ks]
        upd = [jnp.dot(kst[t], vb[t], preferred_element_type=F32) for t in range(len(items))]
        zo = [zo_ref[bb, rows, :] for bb in range(nb)]
        for t, (bb, h) in enumerate(items):
            num = sv[t] + w_i[t] * qc[t]
            den = jnp.sum(s[t], axis=1, keepdims=True) + w_i[t] * jnp.sum(q[t] * n0[t], axis=1, keepdims=True)
            hh = num / jnp.maximum(jnp.abs(den), jnp.exp(-m[t]))
            c_ref[bb, h] = w_c[t] * c0[t] + upd[t]
            n_ref[bb, h:h + 1, :] = w_c[t] * n0[t] + jnp.sum(ks[t], axis=0, keepdims=True)
            m_ref[bb, h] = jnp.broadcast_to(m_t[t], (1, LANES))
            yn = hh * lax.rsqrt(jnp.mean(hh * hh, axis=-1, keepdims=True) + EPS) * gn_ref[...]
            hc_ref[bb, rows, h * C_DV:(h + 1) * C_DV] = \
                (yn * _sigmoid(zo[bb][:, h * C_DV:(h + 1) * C_DV])).astype(BF16)
        return carry

    lax.fori_loop(0, tb // n, chunk, 0)

    @pl.when(ci == pl.num_programs(1) - 1)
    def _():
        for bb in range(nb):
            c1_ref[bb] = c_ref[bb]
            n1_ref[bb] = n_ref[bb]
            m1_ref[bb] = jnp.concatenate([m_ref[bb, h][:, 0:1] for h in range(C_HEADS)], axis=1)


def _mlstm(z3, conv_w, conv_b, ig_b, fg_b, gn, c0, n0, m0, cv0):
    bsz, t, _ = z3.shape
    tb = _row_tile(t, 256)
    assert tb % CHUNK == 0
    width = 512
    gate_bias = jnp.zeros((1, LANES), F32)
    gate_bias = gate_bias.at[0, MISC_CI:MISC_CI + C_HEADS].set(ig_b.astype(F32))
    gate_bias = gate_bias.at[0, MISC_CF:MISC_CF + C_HEADS].set(fg_b.astype(F32))

    nb = MLSTM_BATCH if bsz % MLSTM_BATCH == 0 else 1

    def col(off, w=width):
        return pl.BlockSpec((nb, tb, w), lambda b, i: (b, i, off // w))

    def full(shape):
        nd = len(shape)
        return pl.BlockSpec(shape, lambda b, i: (0,) * nd)

    def per_b(shape):
        nd = len(shape)
        return pl.BlockSpec((nb,) + shape, lambda b, i: (b,) + (0,) * nd)

    return pl.pallas_call(
        functools.partial(_mlstm_kernel, tb=tb, nb=nb),
        grid=(bsz // nb, t // tb),
        in_specs=[col(COL_CQK), col(COL_CV), col(COL_CO), col(COL_MISC, LANES),
                  full((CONV_W, CONV_CH)), full((1, CONV_CH)), full((1, LANES)), full((1, C_DV)),
                  per_b((C_HEADS, C_DK, C_DV)), per_b((C_HEADS, C_DK)), per_b((1, C_HEADS)),
                  per_b((CONV_W - 1, CONV_CH))],
        out_specs=[pl.BlockSpec((nb, tb, width), lambda b, i: (b, i, 0)),
                   per_b((C_HEADS, C_DK, C_DV)), per_b((C_HEADS, C_DK)), per_b((1, C_HEADS))],
        out_shape=[jax.ShapeDtypeStruct((bsz, t, width), BF16),
                   jax.ShapeDtypeStruct((bsz, C_HEADS, C_DK, C_DV), F32),
                   jax.ShapeDtypeStruct((bsz, C_HEADS, C_DK), F32),
                   jax.ShapeDtypeStruct((bsz, 1, C_HEADS), F32)],
        scratch_shapes=[pltpu.VMEM((nb, CONV_PAD + tb, CONV_CH), F32),
                        pltpu.VMEM((nb, tb, CONV_CH), F32),
                        pltpu.VMEM((nb, C_HEADS, C_DK, C_DV), F32),
                        pltpu.VMEM((nb, C_HEADS, C_DK), F32),
                        pltpu.VMEM((nb, C_HEADS, 1, LANES), F32)],
        compiler_params=_cparams("parallel", "arbitrary"),
        name="mlstm",
    )(z3, z3, z3, z3, conv_w, conv_b.reshape(1, CONV_CH), gate_bias, gn.reshape(1, C_DV),
      c0, n0, m0.reshape(bsz, 1, C_HEADS), cv0)


def _out_proj_kernel(oa_ref, ob_ref, hc_ref, x_ref, w_ref, o_ref):
    wa = w_ref[0:1024, :]
    wb = w_ref[1024:1536, :]
    wc = w_ref[1536:2048, :]
    acc = jnp.dot(oa_ref[...], wa, preferred_element_type=F32)
    acc = acc + jnp.dot(ob_ref[...], wb, preferred_element_type=F32)
    acc = acc + jnp.dot(hc_ref[...], wc, preferred_element_type=F32)
    o_ref[...] = x_ref[...] + acc


def _out_proj(oa, ob, hc, x, w):
    n, d = x.shape
    tm = _row_tile(n, 512)
    tn = d
    return pl.pallas_call(
        _out_proj_kernel,
        grid=(n // tm, d // tn),
        in_specs=[pl.BlockSpec((tm, oa.shape[1]), lambda i, j: (i, 0)),
                  pl.BlockSpec((tm, ob.shape[1]), lambda i, j: (i, 0)),
                  pl.BlockSpec((tm, hc.shape[1]), lambda i, j: (i, 0)),
                  pl.BlockSpec((tm, tn), lambda i, j: (i, j)),
                  pl.BlockSpec((w.shape[0], tn), lambda i, j: (0, j))],
        out_specs=pl.BlockSpec((tm, tn), lambda i, j: (i, j)),
        out_shape=jax.ShapeDtypeStruct((n, d), F32),
        compiler_params=_cparams("parallel", "arbitrary"),
        name="out_proj",
    )(oa, ob, hc, x, w)


def _ffn_accumulate(hn_ref, wg_ref, wu_ref, wd_ref, o_ref):
    hn = hn_ref[...]
    a = jnp.dot(hn, wg_ref[0], preferred_element_type=F32)
    b = jnp.dot(hn, wu_ref[0], preferred_element_type=F32)
    hmid = (_silu(a) * b).astype(BF16)
    o_ref[...] += jnp.dot(hmid, wd_ref[0], preferred_element_type=F32)


def _ffn_norm(x, g_ref, hn_ref):
    y = x * lax.rsqrt(jnp.mean(x * x, axis=-1, keepdims=True) + EPS) * g_ref[...]
    hn_ref[...] = y.astype(BF16)


def _ffn_kernel(te_ref, nu_ref, x_ref, g_ref, wg_ref, wu_ref, wd_ref, o_ref, hn_ref):
    @pl.when(pl.program_id(1) == 0)
    def _():
        x = x_ref[...]
        _ffn_norm(x, g_ref, hn_ref)
        o_ref[...] = x

    _ffn_accumulate(hn_ref, wg_ref, wu_ref, wd_ref, o_ref)


def _moe_ffn_kernel(te_ref, nu_ref, idx_ref, idx_next_ref, x_hbm, g_ref, wg_ref, wu_ref, wd_ref, o_ref,
                    xg_ref, hn_ref, sem, *, n_tiles, rows_per_step):
    i = pl.program_id(0)
    f = pl.program_id(1)
    tm = o_ref.shape[0]
    n_used = nu_ref[0]

    def row_copy(ids_ref, slot, r):
        return pltpu.make_async_copy(x_hbm.at[pl.ds(ids_ref[0, 0, r], 1), :],
                                     xg_ref.at[slot, pl.ds(r, 1), :], sem.at[slot])

    def start_rows(ids_ref, slot, lo, count):
        def body(r, c):
            row_copy(ids_ref, slot, lo + r).start()
            return c
        lax.fori_loop(0, count, body, 0, unroll=DMA_UNROLL)

    @pl.when((i == 0) & (f == 0) & (n_used > 0))
    def _():
        start_rows(idx_ref, 0, 0, tm)

    @pl.when((i + 1 < n_tiles) & (i + 1 < n_used) & (f < tm // rows_per_step))
    def _():
        start_rows(idx_next_ref, (i + 1) % 2, f * rows_per_step, rows_per_step)

    @pl.when(i < n_used)
    def _():
        @pl.when(f == 0)
        def _():
            slot = i % 2

            def body(r, c):
                row_copy(idx_ref, slot, r).wait()
                return c
            lax.fori_loop(0, tm, body, 0, unroll=DMA_UNROLL)
            _ffn_norm(xg_ref[slot], g_ref, hn_ref)
            o_ref[...] = jnp.zeros_like(o_ref)

        _ffn_accumulate(hn_ref, wg_ref, wu_ref, wd_ref, o_ref)

    @pl.when((i >= n_used) & (f == 0))
    def _():
        o_ref[...] = jnp.zeros_like(o_ref)


def _ffn_specs(d, tf):
    def w_in_map(i, f, te, nu):
        return (te[i], 0, jnp.where(i < nu[0], f, 0))

    def w_out_map(i, f, te, nu):
        return (te[i], jnp.where(i < nu[0], f, 0), 0)

    return [pl.BlockSpec((1, d), lambda i, f, te, nu: (0, 0)),
            pl.BlockSpec((1, d, tf), w_in_map),
            pl.BlockSpec((1, d, tf), w_in_map),
            pl.BlockSpec((1, tf, d), w_out_map)]


def _ffn_dense(x, g, wg, wu, wd):
    n, d = x.shape
    ff = wg.shape[2]
    tm = _row_tile(n, 512)
    assert ff % FFN_COLS == 0
    grid_spec = pltpu.PrefetchScalarGridSpec(
        num_scalar_prefetch=2,
        grid=(n // tm, ff // FFN_COLS),
        in_specs=[pl.BlockSpec((tm, d), lambda i, f, te, nu: (i, 0))] + _ffn_specs(d, FFN_COLS),
        out_specs=pl.BlockSpec((tm, d), lambda i, f, te, nu: (i, 0)),
        scratch_shapes=[pltpu.VMEM((tm, d), BF16)],
    )
    return pl.pallas_call(
        _ffn_kernel,
        grid_spec=grid_spec,
        out_shape=jax.ShapeDtypeStruct((n, d), F32),
        compiler_params=_cparams("parallel", "arbitrary"),
        name="swiglu_ffn",
    )(jnp.zeros((n // tm,), I32), jnp.full((1,), n // tm, I32), x, g.reshape(1, d), wg, wu, wd)


def _ffn_moe(x, src_tok, g, wg, wu, wd, tile_expert, n_used, tm):
    d = x.shape[1]
    p = src_tok.shape[0]
    ff = wg.shape[2]
    n_tiles = p // tm
    nf = ff // FFN_COLS
    assert ff % FFN_COLS == 0 and p % tm == 0
    issue_steps = 1
    while issue_steps * 2 <= min(nf, 8):
        issue_steps *= 2
    ids = src_tok.reshape(n_tiles, 1, tm)
    grid_spec = pltpu.PrefetchScalarGridSpec(
        num_scalar_prefetch=2,
        grid=(n_tiles, nf),
        in_specs=[pl.BlockSpec((1, 1, tm), lambda i, f, te, nu: (i, 0, 0), memory_space=pltpu.SMEM),
                  pl.BlockSpec((1, 1, tm), lambda i, f, te, nu: (jnp.minimum(i + 1, n_tiles - 1), 0, 0),
                               memory_space=pltpu.SMEM),
                  pl.BlockSpec(memory_space=pl.ANY)] + _ffn_specs(d, FFN_COLS),
        out_specs=pl.BlockSpec((tm, d), lambda i, f, te, nu: (i, 0)),
        scratch_shapes=[pltpu.VMEM((2, tm, d), F32), pltpu.VMEM((tm, d), BF16), pltpu.SemaphoreType.DMA((2,))],
    )
    return pl.pallas_call(
        functools.partial(_moe_ffn_kernel, n_tiles=n_tiles, rows_per_step=tm // issue_steps),
        grid_spec=grid_spec,
        out_shape=jax.ShapeDtypeStruct((p, d), F32),
        compiler_params=_cparams("arbitrary", "arbitrary"),
        name="moe_ffn",
    )(tile_expert, n_used, ids, ids, x, g.reshape(1, d), wg, wu, wd)


def _router_kernel(x_ref, g_ref, w_ref, b_ref, e_ref, p_ref):
    x = x_ref[...]
    h = x * lax.rsqrt(jnp.mean(x * x, axis=-1, keepdims=True) + EPS) * g_ref[...]
    logits = jnp.dot(h, w_ref[...], precision=lax.Precision.HIGHEST, preferred_element_type=F32) + b_ref[...]
    lane = lax.broadcasted_iota(I32, logits.shape, 1)
    logits = jnp.where(lane < N_EXPERTS, logits, -jnp.inf)
    m1 = jnp.max(logits, axis=1, keepdims=True)
    i1 = jnp.min(jnp.where(logits == m1, lane, LANES), axis=1, keepdims=True)
    rest = jnp.where(lane == i1, -jnp.inf, logits)
    m2 = jnp.max(rest, axis=1, keepdims=True)
    i2 = jnp.min(jnp.where(rest == m2, lane, LANES), axis=1, keepdims=True)
    e2 = jnp.exp(m2 - m1)
    den = 1.0 + e2
    e_ref[...] = jnp.where(lane == 0, i1, jnp.where(lane == 1, i2, 0))
    p_ref[...] = jnp.where(lane == 0, 1.0 / den, jnp.where(lane == 1, e2 / den, 0.0))


def _router(x, g, w_router, b_router):
    n, d = x.shape
    tm = _row_tile(n, 512)
    w = jnp.zeros((d, LANES), F32).at[:, :N_EXPERTS].set(w_router.astype(F32))
    b = jnp.zeros((1, LANES), F32).at[0, :N_EXPERTS].set(b_router.astype(F32))
    return pl.pallas_call(
        _router_kernel,
        grid=(n // tm,),
        in_specs=[pl.BlockSpec((tm, d), lambda i: (i, 0)),
                  pl.BlockSpec((1, d), lambda i: (0, 0)),
                  pl.BlockSpec((d, LANES), lambda i: (0, 0)),
                  pl.BlockSpec((1, LANES), lambda i: (0, 0))],
        out_specs=[pl.BlockSpec((tm, LANES), lambda i: (i, 0)),
                   pl.BlockSpec((tm, LANES), lambda i: (i, 0))],
        out_shape=[jax.ShapeDtypeStruct((n, LANES), I32),
                   jax.ShapeDtypeStruct((n, LANES), F32)],
        compiler_params=_cparams("parallel"),
        name="moe_router",
    )(x, g.reshape(1, d), w, b)


def _combine_kernel(i1_ref, i2_ref, x_ref, p_ref, yb_ref, o_ref, r1_ref, r2_ref, sem):
    rows = o_ref.shape[0]

    def row_copy(idx_ref, dst_ref, r):
        return pltpu.make_async_copy(yb_ref.at[pl.ds(idx_ref[0, 0, r], 1), :], dst_ref.at[pl.ds(r, 1), :], sem)

    def start(r, c):
        row_copy(i1_ref, r1_ref, r).start()
        row_copy(i2_ref, r2_ref, r).start()
        return c

    def wait(r, c):
        row_copy(i1_ref, r1_ref, r).wait()
        row_copy(i2_ref, r2_ref, r).wait()
        return c

    lax.fori_loop(0, rows, start, 0, unroll=DMA_UNROLL)
    lax.fori_loop(0, rows, wait, 0, unroll=DMA_UNROLL)
    p = p_ref[...]
    o_ref[...] = x_ref[...] + (r1_ref[...] * p[:, 0:1] + r2_ref[...] * p[:, 1:2])


def _combine(x, yb, i1, i2, gates):
    n, d = x.shape
    rows = _row_tile(n, GATHER_ROWS)
    row = pl.BlockSpec((rows, d), lambda i: (i, 0))
    idx = pl.BlockSpec((1, 1, rows), lambda i: (i, 0, 0), memory_space=pltpu.SMEM)
    return pl.pallas_call(
        _combine_kernel,
        grid=(n // rows,),
        in_specs=[idx, idx, row, pl.BlockSpec((rows, LANES), lambda i: (i, 0)),
                  pl.BlockSpec(memory_space=pl.ANY)],
        out_specs=row,
        out_shape=jax.ShapeDtypeStruct((n, d), F32),
        scratch_shapes=[pltpu.VMEM((rows, d), F32), pltpu.VMEM((rows, d), F32), pltpu.SemaphoreType.DMA(())],
        compiler_params=_cparams("arbitrary"),
        name="moe_combine",
    )(i1.reshape(n // rows, 1, rows), i2.reshape(n // rows, 1, rows), x, gates, yb)


def _moe(x, g, w_router, b_router, wg, wu, wd):
    n, d = x.shape
    e_out, gates = _router(x, g, w_router, b_router)
    flat_e = e_out[:, :TOP_K].reshape(n * TOP_K)
    nk = n * TOP_K
    tm = 512 if nk >= 8192 else 128
    onehot = (flat_e[:, None] == jnp.arange(N_EXPERTS, dtype=I32)[None, :]).astype(I32)
    csum = jnp.cumsum(onehot, axis=0)
    counts = csum[-1]
    rank = jnp.sum((csum - 1) * onehot, axis=1)
    padded = (counts + tm - 1) // tm * tm
    pad_end = jnp.cumsum(padded)
    pad_start = pad_end - padded
    dest = (pad_start[flat_e] + rank).astype(I32)
    n_tiles = (nk + N_EXPERTS * (tm - 1) + tm - 1) // tm
    p = n_tiles * tm
    src_tok = jnp.zeros((p,), I32).at[dest].set(jnp.arange(nk, dtype=I32) // TOP_K)
    tile_expert = jnp.minimum(jnp.searchsorted(pad_end, jnp.arange(n_tiles, dtype=I32) * tm, side='right'),
                              N_EXPERTS - 1).astype(I32)
    n_used = (pad_end[-1] // tm).astype(I32).reshape(1)

    yb = _ffn_moe(x, src_tok, g, wg, wu, wd, tile_expert, n_used, tm)
    dest2 = dest.reshape(n, TOP_K)
    return _combine(x, yb, dest2[:, 0], dest2[:, 1], gates)


def _reorder_w_in(w):
    offs, o = {}, 0
    for name, width in (("a_q", 1024), ("a_k", 256), ("a_v", 256), ("i_q", 512), ("i_w", 8), ("i_k", 64),
                        ("b_q", 512), ("b_f", 512), ("b_i", 512), ("b_g", 512), ("c_qk", 512), ("c_v", 512),
                        ("c_i", 4), ("c_f", 4), ("c_o", 512)):
        offs[name] = (o, width)
        o += width

    def c(name):
        s, width = offs[name]
        return w[:, s:s + width]

    pad = jnp.zeros((w.shape[0], LANES - (IDX_DIM + IDX_HEADS + 2 * C_HEADS)), w.dtype)
    cols = [c("a_q"), c("a_k"), c("a_v"), c("i_q"), c("b_q"), c("b_f"), c("b_i"), c("b_g"),
            c("c_qk"), c("c_v"), c("c_o"), c("i_k"), c("i_w"), c("c_i"), c("c_f"), pad]
    return jnp.concatenate(cols, axis=1).astype(BF16)


def _trunk(x, pos0, W, lbs, past):
    bsz, t, d = x.shape
    n = bsz * t
    pos = pos0 + jnp.arange(t, dtype=I32)
    xf = x.reshape(n, d)
    states = []
    depth = W['w_in'].shape[0]
    for l in range(depth):
        z = _norm_matmul(xf, W['attn_norm'][l], W['w_in_r'][l], tn=1152)
        z3 = z.reshape(bsz, t, Z_WIDTH)
        q, k, qi, ki = _prep(z3, W['q_norm'][l], W['k_norm'][l], pos)
        v = z3[:, :, COL_AV:COL_AV + 256]
        if past is None:
            n_keys = t
            assert t % KEY_BLOCK == 0
            o_a = _sparse_attention(q, qi, z3, k, z3, COL_AV // 256, ki, n_keys, 0)
            s0 = jnp.zeros((bsz, B_HEADS, B_DK, B_DV), F32)
            c0 = jnp.zeros((bsz, C_HEADS, C_DK, C_DV), F32)
            n0 = jnp.zeros((bsz, C_HEADS, C_DK), F32)
            m0 = jnp.zeros((bsz, C_HEADS), F32)
            cv0 = jnp.zeros((bsz, CONV_W - 1, CONV_CH), F32)
        else:
            pk = past['k'][l].reshape(bsz, -1, 256)
            pv = past['v'][l].reshape(bsz, -1, 256)
            n_keys = pk.shape[1] + t
            lp = (n_keys + KEY_BLOCK - 1) // KEY_BLOCK * KEY_BLOCK
            padk = jnp.zeros((bsz, lp - n_keys, 256), F32)
            k_all = jnp.concatenate([pk, k, padk], axis=1)
            v_all = jnp.concatenate([pv, v, padk], axis=1)
            ki_all = jnp.concatenate([past['ki'][l], ki, padk[:, :, :IDX_DIM]], axis=1)
            o_a = _sparse_attention(q, qi, z3, k_all, v_all, 0, ki_all, n_keys, pk.shape[1])
            s0, c0, n0, m0, cv0 = past['hgrn'][l], past['C'][l], past['n'][l], past['m'][l], past['conv'][l]
        ob, s1 = _hgrn(z3, lbs[l], W['hgrn_norm'][l], s0)
        hc, c1, n1, m1 = _mlstm(z3, W['conv_w'][l], W['conv_b'][l], W['ig_b'][l], W['fg_b'][l],
                                W['mlstm_norm'][l], c0, n0, m0, cv0)
        cqk = z3[:, :, COL_CQK:COL_CQK + CONV_CH]
        conv1 = jnp.concatenate([cv0, cqk], axis=1)[:, -(CONV_W - 1):]
        xf = _out_proj(o_a.reshape(n, -1), ob.reshape(n, -1), hc.reshape(n, -1), xf, W['w_out_b'][l])
        jj = l // 2
        if l % 2 == 0:
            xf = _ffn_dense(xf, W['ffn_norm'][l], W['dense_wg_b'][jj][None], W['dense_wu_b'][jj][None],
                            W['dense_wd_b'][jj][None])
        else:
            xf = _moe(xf, W['ffn_norm'][l], W['moe_router'][jj], W['moe_router_b'][jj],
                      W['moe_wg_b'][jj], W['moe_wu_b'][jj], W['moe_wd_b'][jj])
        states.append(dict(k=k.reshape(bsz, t, A_KV_HEADS, HEAD_DIM), v=v.reshape(bsz, t, A_KV_HEADS, HEAD_DIM),
                           ki=ki, hgrn=s1, C=c1, n=n1, m=m1.reshape(bsz, C_HEADS), conv=conv1))
    stacked = {name: jnp.stack([s[name] for s in states]) for name in states[0]}
    return xf.reshape(bsz, t, d), stacked


def kernel(x_prompt, x_sample, cache_k, cache_v, cache_kidx, state_hgrn, state_mlstm_C, state_mlstm_n,
           state_mlstm_m, state_mlstm_conv, attn_norm, w_in, w_out, q_norm, k_norm, hgrn_lb_logits,
           hgrn_norm, conv_w, conv_b, ig_b, fg_b, mlstm_norm, ffn_norm, dense_wg, dense_wu, dense_wd,
           moe_router, moe_router_b, moe_wg, moe_wu, moe_wd):
    depth = w_in.shape[0]
    W = dict(attn_norm=attn_norm, w_in=w_in, q_norm=q_norm, k_norm=k_norm, hgrn_norm=hgrn_norm,
             conv_w=conv_w, conv_b=conv_b, ig_b=ig_b, fg_b=fg_b, mlstm_norm=mlstm_norm, ffn_norm=ffn_norm,
             moe_router=moe_router, moe_router_b=moe_router_b,
             w_in_r=jnp.stack([_reorder_w_in(w_in[l]) for l in range(depth)]),
             w_out_b=w_out.astype(BF16),
             dense_wg_b=dense_wg.astype(BF16), dense_wu_b=dense_wu.astype(BF16), dense_wd_b=dense_wd.astype(BF16),
             moe_wg_b=moe_wg.astype(BF16), moe_wu_b=moe_wu.astype(BF16), moe_wd_b=moe_wd.astype(BF16))
    p_lb = jax.nn.softmax(hgrn_lb_logits.astype(F32), axis=0)
    lbs = jnp.cumsum(p_lb, axis=0) - p_lb[0]
    y_prompt, sp = _trunk(x_prompt, 0, W, lbs, None)
    past = dict(k=cache_k, v=cache_v, ki=cache_kidx, hgrn=state_hgrn, C=state_mlstm_C, n=state_mlstm_n,
                m=state_mlstm_m, conv=state_mlstm_conv)
    y_sample, ss = _trunk(x_sample, cache_k.shape[2], W, lbs, past)
    return (y_prompt, y_sample,
            sp['k'], sp['v'], sp['ki'], sp['hgrn'], sp['C'], sp['n'], sp['m'], sp['conv'],
            ss['k'], ss['v'], ss['ki'], ss['hgrn'], ss['C'], ss['n'], ss['m'], ss['conv'])
```

```python
import functools
import math

import numpy as np
import jax
import jax.numpy as jnp
from jax import lax
from jax.experimental import pallas as pl
from jax.experimental.pallas import tpu as pltpu

F32 = jnp.float32
BF16 = jnp.bfloat16
I32 = jnp.int32

CHUNK = 64
EPS = 1e-6
NEG = -1e30
HEAD_DIM = 128
A_HEADS = 8
A_KV_HEADS = 2
GQA = A_HEADS // A_KV_HEADS
IDX_HEADS = 8
IDX_DIM = 64
TOPK_MAX = 256
ROPE_THETA = 500000.0
ROT_FRAC = 4
ATTN_SCALE = HEAD_DIM ** -0.5
IDX_SCALE = (IDX_HEADS * IDX_DIM) ** -0.5
B_HEADS = 4
B_DK = 128
B_DV = 128
C_HEADS = 4
C_DK = 64
C_DV = 128
CONV_W = 4
CONV_CH = 2 * C_HEADS * C_DK
N_EXPERTS = 8
TOP_K = 2

COL_AQ, COL_AK, COL_AV, COL_IQ = 0, 1024, 1280, 1536
COL_BQ, COL_BF, COL_BI, COL_BG = 2048, 2560, 3072, 3584
COL_CQK, COL_CV, COL_CO, COL_MISC = 4096, 4608, 5120, 5632
Z_WIDTH = 5760
MISC_IK, MISC_IW, MISC_CI, MISC_CF = 0, 64, 72, 76

LANES = 128
VMEM_LIMIT_BYTES = 56 * 1024 * 1024
INT_MIN = -2 ** 31

HGRN_SUB = 8
KEY_BLOCK = 1024
ATTN_ROWS = 256
GATHER_ROWS = 256
FFN_COLS = 512
DMA_UNROLL = 8
MLSTM_BATCH = 2


def _cparams(*sem):
    return pltpu.CompilerParams(dimension_semantics=sem, vmem_limit_bytes=VMEM_LIMIT_BYTES)


def _sigmoid(x):
    return 1.0 / (1.0 + jnp.exp(-x))


def _silu(x):
    return x * _sigmoid(x)


def _row_tile(n, pref):
    t = min(n, pref)
    while n % t:
        t //= 2
    return t


def _norm_matmul_kernel(x_ref, g_ref, w_ref, o_ref, xn_ref):
    @pl.when(pl.program_id(1) == 0)
    def _():
        x = x_ref[...]
        y = x * lax.rsqrt(jnp.mean(x * x, axis=-1, keepdims=True) + EPS) * g_ref[...]
        xn_ref[...] = y.astype(BF16)

    o_ref[...] = jnp.dot(xn_ref[...], w_ref[...], preferred_element_type=F32)


def _norm_matmul(x, g, w, tn):
    n, d = x.shape
    nout = w.shape[1]
    tm = _row_tile(n, 1024)
    return pl.pallas_call(
        _norm_matmul_kernel,
        grid=(n // tm, nout // tn),
        in_specs=[pl.BlockSpec((tm, d), lambda i, j: (i, 0)),
                  pl.BlockSpec((1, d), lambda i, j: (0, 0)),
                  pl.BlockSpec((d, tn), lambda i, j: (0, j))],
        out_specs=pl.BlockSpec((tm, tn), lambda i, j: (i, j)),
        out_shape=jax.ShapeDtypeStruct((n, nout), F32),
        scratch_shapes=[pltpu.VMEM((tm, d), BF16)],
        compiler_params=_cparams("parallel", "arbitrary"),
        name="norm_in_proj",
    )(x, g.reshape(1, d), w)


def _rope(x, c, sa, sb, half):
    w = x.shape[-1]
    return x * c + pltpu.roll(x, w - half, 1) * sa + pltpu.roll(x, half, 1) * sb


def _prep_kernel(zq_ref, zk_ref, ziq_ref, zm_ref, qn_ref, kn_ref, t128_ref, t64_ref,
                 q_ref, k_ref, qi_ref, ki_ref):
    c1, sa1, sb1 = t128_ref[0], t128_ref[1], t128_ref[2]
    c2, sa2, sb2 = t64_ref[0], t64_ref[1], t64_ref[2]
    half1 = HEAD_DIM // ROT_FRAC // 2
    half2 = IDX_DIM // ROT_FRAC // 2

    def normed(x, g):
        return x * lax.rsqrt(jnp.mean(x * x, axis=-1, keepdims=True) + EPS) * g

    for h in range(A_HEADS):
        x = zq_ref[0, :, h * HEAD_DIM:(h + 1) * HEAD_DIM]
        q_ref[0, h] = _rope(normed(x, qn_ref[...]), c1, sa1, sb1, half1)
    for h in range(A_KV_HEADS):
        x = zk_ref[0, :, h * HEAD_DIM:(h + 1) * HEAD_DIM]
        k_ref[0, :, h * HEAD_DIM:(h + 1) * HEAD_DIM] = _rope(normed(x, kn_ref[...]), c1, sa1, sb1, half1)
    for j in range(IDX_HEADS * IDX_DIM // LANES):
        x = ziq_ref[0, :, j * LANES:(j + 1) * LANES]
        qi_ref[0, :, j * LANES:(j + 1) * LANES] = _rope(x, c2, sa2, sb2, half2)
    m = _rope(zm_ref[0], c2, sa2, sb2, half2)
    ki_ref[0] = m[:, MISC_IK:MISC_IK + IDX_DIM]


def _rope_tables(pos, period):
    rot = period // ROT_FRAC
    half = rot // 2
    inv = jnp.exp(-math.log(ROPE_THETA) * 2.0 * jnp.arange(half, dtype=F32) / rot)
    ang = pos.astype(F32)[:, None] * inv
    cos, sin = jnp.cos(ang), jnp.sin(ang)
    t = pos.shape[0]
    ones = jnp.ones((t, period - rot), F32)
    zeros_h = jnp.zeros((t, half), F32)
    zeros_r = jnp.zeros((t, period - rot), F32)
    c = jnp.concatenate([cos, cos, ones], axis=1)
    sa = jnp.concatenate([-sin, zeros_h, zeros_r], axis=1)
    sb = jnp.concatenate([zeros_h, sin, zeros_r], axis=1)
    tab = jnp.stack([c, sa, sb])
    return jnp.tile(tab, (1, 1, LANES // period))


def _prep(z3, q_norm, k_norm, pos):
    bsz, t, _ = z3.shape
    tt = _row_tile(t, 256)
    t128 = _rope_tables(pos, HEAD_DIM)
    t64 = _rope_tables(pos, IDX_DIM)

    def col(width, off):
        return pl.BlockSpec((1, tt, width), lambda b, i: (b, i, off // width))

    return pl.pallas_call(
        _prep_kernel,
        grid=(bsz, t // tt),
        in_specs=[col(1024, COL_AQ), col(256, COL_AK), col(512, COL_IQ), col(128, COL_MISC),
                  pl.BlockSpec((1, HEAD_DIM), lambda b, i: (0, 0)),
                  pl.BlockSpec((1, HEAD_DIM), lambda b, i: (0, 0)),
                  pl.BlockSpec((3, tt, LANES), lambda b, i: (0, i, 0)),
                  pl.BlockSpec((3, tt, LANES), lambda b, i: (0, i, 0))],
        out_specs=[pl.BlockSpec((1, A_HEADS, tt, HEAD_DIM), lambda b, i: (b, 0, i, 0)),
                   pl.BlockSpec((1, tt, A_KV_HEADS * HEAD_DIM), lambda b, i: (b, i, 0)),
                   pl.BlockSpec((1, tt, IDX_HEADS * IDX_DIM), lambda b, i: (b, i, 0)),
                   pl.BlockSpec((1, tt, IDX_DIM), lambda b, i: (b, i, 0))],
        out_shape=[jax.ShapeDtypeStruct((bsz, A_HEADS, t, HEAD_DIM), F32),
                   jax.ShapeDtypeStruct((bsz, t, A_KV_HEADS * HEAD_DIM), F32),
                   jax.ShapeDtypeStruct((bsz, t, IDX_HEADS * IDX_DIM), F32),
                   jax.ShapeDtypeStruct((bsz, t, IDX_DIM), F32)],
        compiler_params=_cparams("parallel", "parallel"),
        name="qk_prep",
    )(z3, z3, z3, z3, q_norm.reshape(1, HEAD_DIM), k_norm.reshape(1, HEAD_DIM), t128, t64)


def _f2key(x):
    bits = lax.bitcast_convert_type(x, I32)
    return jnp.where(bits < 0, bits ^ 0x7FFFFFFF, bits)


def _attn_kernel(q_ref, qi_ref, zm_ref, k_ref, v_ref, ki_ref, o_ref,
                 key_ref, bias_ref, qs_ref, kb_ref, vb_ref, kib_ref, acc_ref, m_ref, l_ref, *,
                 tq, n_keys, topk, q_pos0, kb_size, lp):
    j = pl.program_id(1)
    q_first = q_pos0 + j * tq
    lim = jnp.minimum(((q_first + tq - 1) // CHUNK + 1) * CHUNK, n_keys)
    n_kb = (lim + kb_size - 1) // kb_size

    @pl.when(j == 0)
    def _():
        def cast_body(kb, c):
            sl = pl.ds(pl.multiple_of(kb * kb_size, kb_size), kb_size)
            kb_ref[sl, :] = k_ref[0, sl, :].astype(BF16)
            vb_ref[sl, :] = v_ref[0, sl, :].astype(BF16)
            kib_ref[sl, :] = ki_ref[0, sl, :].astype(BF16)
            return c
        lax.fori_loop(0, lp // kb_size, cast_body, 0)

    for h in range(A_HEADS):
        qs_ref[h // GQA, (h % GQA) * tq:(h % GQA + 1) * tq, :] = \
            (q_ref[0, h] * (ATTN_SCALE * math.log2(math.e))).astype(BF16)

    row = lax.broadcasted_iota(I32, (tq, kb_size), 0)
    lane = lax.broadcasted_iota(I32, (tq, kb_size), 1)
    q_chunk = lax.shift_right_logical(q_first + row, 6)

    qi = qi_ref[0].astype(BF16)
    wi = zm_ref[0][:, MISC_IW:MISC_IW + IDX_HEADS] * IDX_SCALE

    def score_body(kb, carry):
        k0 = pl.multiple_of(kb * kb_size, kb_size)
        ki_blk = kib_ref[pl.ds(k0, kb_size), :]
        s = jnp.zeros((tq, kb_size), F32)
        for h in range(IDX_HEADS):
            r = lax.dot_general(qi[:, h * IDX_DIM:(h + 1) * IDX_DIM], ki_blk,
                                (((1,), (1,)), ((), ())), preferred_element_type=F32)
            s = s + jnp.maximum(r, 0.0) * wi[:, h:h + 1]
        kidx = k0 + lane
        admiss = (lax.shift_right_logical(kidx, 6) <= q_chunk) & (kidx < n_keys)
        key_ref[kb] = _f2key(jnp.where(admiss, s, NEG))
        return carry

    lax.fori_loop(0, n_kb, score_body, 0)

    def count(pred_fn):
        def body(kb, acc):
            m = pred_fn(key_ref[kb], kb).astype(I32)
            for c in range(kb_size // LANES):
                acc = acc + m[:, c * LANES:(c + 1) * LANES]
            return acc
        acc = lax.fori_loop(0, n_kb, body, jnp.zeros((tq, LANES), I32))
        return jnp.sum(acc, axis=1, keepdims=True)

    q_chunk1 = lax.shift_right_logical(q_first + lax.broadcasted_iota(I32, (tq, 1), 0), 6)
    n_adm = jnp.minimum((q_chunk1 + 1) * CHUNK, n_keys)
    neg_key = int(np.float32(NEG).view(np.int32)) ^ 0x7FFFFFFF
    few = n_adm <= topk

    def search_cond(st):
        step, prefix, thr, c_thr, done = st
        return (step < 32) & (jnp.min(done) == 0)

    def search_body(st):
        step, prefix, thr, c_thr, done = st
        cand = prefix ^ lax.shift_left(jnp.int32(1), 31 - step)
        c = count(lambda key, kb: key >= cand)
        ge = c >= topk
        hit = (c == topk) & (done == 0)
        return (step + 1, jnp.where(ge, cand, prefix), jnp.where(hit, cand, thr),
                jnp.where(hit, c, c_thr), jnp.where(hit, 1, done))

    init = (jnp.int32(0), jnp.full((tq, 1), INT_MIN, I32), jnp.full((tq, 1), neg_key, I32),
            jnp.full((tq, 1), topk, I32), few.astype(I32))
    step_f, prefix_f, thr, c_ge, done = lax.while_loop(search_cond, search_body, init)
    thr = jnp.where(done != 0, thr, prefix_f)
    c_ge = jnp.where(done != 0, c_ge, topk + 1)

    def tie_limit():
        need = topk - count(lambda key, kb: key > thr)

        def idx_body(i, x):
            cand = x | lax.shift_left(jnp.int32(1), 14 - i)
            c = count(lambda key, kb: (key == thr) & (kb * kb_size + lane < cand))
            return jnp.where(c < need, cand, x)
        return lax.fori_loop(0, 15, idx_body, jnp.zeros((tq, 1), I32))

    xlim = lax.cond(jnp.max(c_ge) > topk, tie_limit, lambda: jnp.full((tq, 1), 2 ** 30, I32))

    valid_key = int(np.float32(0.5 * NEG).view(np.int32)) ^ 0x7FFFFFFF

    def mask_body(kb, carry):
        key = key_ref[kb]
        sel = (key > thr) | ((key == thr) & (kb * kb_size + lane <= xlim))
        bias_ref[kb] = jnp.where(sel & (key > valid_key), 0.0, NEG)
        return carry

    lax.fori_loop(0, n_kb, mask_body, 0)

    n_col = kb_size // LANES

    def logits(g, kb):
        sl = pl.ds(pl.multiple_of(kb * kb_size, kb_size), kb_size)
        k_blk = kb_ref[sl, g * HEAD_DIM:(g + 1) * HEAD_DIM]
        s = lax.dot_general(qs_ref[g], k_blk, (((1,), (1,)), ((), ())), preferred_element_type=F32)
        return s.reshape(GQA, tq, kb_size) + bias_ref[kb][None]

    for g in range(A_KV_HEADS):
        def max_body(kb, mx, g=g):
            s = logits(g, kb)
            for c in range(n_col):
                mx = jnp.maximum(mx, s[:, :, c * LANES:(c + 1) * LANES])
            return mx

        mx = lax.fori_loop(0, n_kb, max_body, jnp.full((GQA, tq, LANES), NEG, F32))
        m_ref[...] = jnp.broadcast_to(jnp.max(mx, axis=2, keepdims=True), (GQA, tq, LANES))
        l_ref[...] = jnp.zeros(l_ref.shape, F32)
        acc_ref[...] = jnp.zeros(acc_ref.shape, F32)

        def att_body(kb, carry, g=g):
            sl = pl.ds(pl.multiple_of(kb * kb_size, kb_size), kb_size)
            v_blk = vb_ref[sl, g * HEAD_DIM:(g + 1) * HEAD_DIM]
            mb = m_ref[...]
            p = jnp.exp2(logits(g, kb) - jnp.concatenate([mb] * n_col, axis=2))
            lsum = l_ref[...]
            for c in range(n_col):
                lsum = lsum + p[:, :, c * LANES:(c + 1) * LANES]
            l_ref[...] = lsum
            pv = jnp.dot(p.reshape(GQA * tq, kb_size).astype(BF16), v_blk, preferred_element_type=F32)
            acc_ref[...] += pv.reshape(GQA, tq, HEAD_DIM)
            return carry

        lax.fori_loop(0, n_kb, att_body, 0)
        out = acc_ref[...] / jnp.sum(l_ref[...], axis=2, keepdims=True)
        for hh in range(GQA):
            h = g * GQA + hh
            o_ref[0, :, h * HEAD_DIM:(h + 1) * HEAD_DIM] = out[hh].astype(BF16)


def _sparse_attention(q, qi, z3, k_all, v_src, v_col, ki_all, n_keys, q_pos0):
    bsz, _, t, _ = q.shape
    lp = k_all.shape[1]
    tq = _row_tile(t, ATTN_ROWS)
    topk = min(TOPK_MAX, n_keys // 4)
    kb_size = KEY_BLOCK
    assert lp % kb_size == 0 and kb_size >= topk
    kern = functools.partial(_attn_kernel, tq=tq, n_keys=n_keys, topk=topk, q_pos0=q_pos0, kb_size=kb_size,
                             lp=lp)
    return pl.pallas_call(
        kern,
        grid=(bsz, t // tq),
        in_specs=[pl.BlockSpec((1, A_HEADS, tq, HEAD_DIM), lambda b, i: (b, 0, i, 0)),
                  pl.BlockSpec((1, tq, IDX_HEADS * IDX_DIM), lambda b, i: (b, i, 0)),
                  pl.BlockSpec((1, tq, LANES), lambda b, i: (b, i, COL_MISC // LANES)),
                  pl.BlockSpec((1, lp, 256), lambda b, i: (b, 0, 0)),
                  pl.BlockSpec((1, lp, 256), lambda b, i: (b, 0, v_col)),
                  pl.BlockSpec((1, lp, IDX_DIM), lambda b, i: (b, 0, 0))],
        out_specs=pl.BlockSpec((1, tq, A_HEADS * HEAD_DIM), lambda b, i: (b, i, 0)),
        out_shape=jax.ShapeDtypeStruct((bsz, t, A_HEADS * HEAD_DIM), BF16),
        scratch_shapes=[pltpu.VMEM((lp // kb_size, tq, kb_size), I32),
                        pltpu.VMEM((lp // kb_size, tq, kb_size), F32),
                        pltpu.VMEM((A_KV_HEADS, GQA * tq, HEAD_DIM), BF16),
                        pltpu.VMEM((lp, A_KV_HEADS * HEAD_DIM), BF16),
                        pltpu.VMEM((lp, A_KV_HEADS * HEAD_DIM), BF16),
                        pltpu.VMEM((lp, IDX_DIM), BF16),
                        pltpu.VMEM((GQA, tq, HEAD_DIM), F32),
                        pltpu.VMEM((GQA, tq, LANES), F32),
                        pltpu.VMEM((GQA, tq, LANES), F32)],
        compiler_params=_cparams("parallel", "arbitrary"),
        name="sparse_attention",
    )(q, qi, z3, k_all, v_src, ki_all)


def _hgrn_kernel(zq_ref, zf_ref, zi_ref, zg_ref, lb_ref, gn_ref, s0_ref, ob_ref, s1_ref, st_ref, *, tb):
    c = pl.program_id(1)
    sub = HGRN_SUB

    @pl.when(c == 0)
    def _():
        for h in range(B_HEADS):
            st_ref[h] = s0_ref[0, h].T

    n = CHUNK
    ns = n // sub
    rr = lax.broadcasted_iota(I32, (n, n), 0)
    cc = lax.broadcasted_iota(I32, (n, n), 1)
    tril = (rr >= cc).astype(F32)
    rowi = lax.broadcasted_iota(I32, (n, B_DK), 0)
    t_in = lax.broadcasted_iota(I32, (ns, sub, B_DK), 1)
    nt = (((1,), (1,)), ((), ()))

    def step(i, carry):
        r0 = pl.multiple_of(i * n, n)
        zq = zq_ref[0, pl.ds(r0, n), :]
        zf = zf_ref[0, pl.ds(r0, n), :]
        zi = zi_ref[0, pl.ds(r0, n), :]
        zg = zg_ref[0, pl.ds(r0, n), :]
        heads = range(B_HEADS)
        sls = [slice(h * B_DK, (h + 1) * B_DK) for h in heads]
        k, q, v, vb, b = [], [], [], [], []
        for h in heads:
            lb = lb_ref[:, sls[h]]
            f = lb + (1.0 - lb) * _sigmoid(zf[:, sls[h]])
            logf = jnp.log(jnp.maximum(f, 1e-30))
            k.append(1.0 - f)
            q.append(_silu(zq[:, sls[h]]))
            v.append(zi[:, sls[h]])
            vb.append(v[h].astype(BF16))
            b.append(jnp.dot(tril, logf, precision=lax.Precision.HIGHEST, preferred_element_type=F32))
        st = [st_ref[h] for h in heads]
        o_mm = [lax.dot_general((q[h] * jnp.exp(b[h])).astype(BF16), st[h].astype(BF16), nt,
                                preferred_element_type=F32) for h in heads]
        upd = []
        for h in heads:
            strips = [jnp.zeros((sub, n), F32)]
            for j in range(1, ns):
                ref = b[h][j * sub - 1:j * sub, :]
                qt = q[h][j * sub:(j + 1) * sub] * jnp.exp(b[h][j * sub:(j + 1) * sub] - ref)
                kt = k[h] * jnp.exp(jnp.where(rowi < j * sub, ref - b[h], NEG))
                strips.append(lax.dot_general(qt.astype(BF16), kt.astype(BF16), nt,
                                              preferred_element_type=F32))
            o_mm[h] = o_mm[h] + jnp.dot(jnp.concatenate(strips, axis=0).astype(BF16), vb[h],
                                        preferred_element_type=F32)
            kt = k[h] * jnp.exp(b[h][n - 1:n, :] - b[h])
            upd.append(lax.dot_general(vb[h], kt.astype(BF16), (((0,), (0,)), ((), ())),
                                       preferred_element_type=F32))
        for h in heads:
            b3 = b[h].reshape(ns, sub, B_DK)
            q3 = q[h].reshape(ns, sub, B_DK)
            k3 = k[h].reshape(ns, sub, B_DK)
            v3 = v[h].reshape(ns, sub, B_DV)
            od = jnp.zeros((ns, sub, B_DV), F32)
            for s in range(sub):
                w = jnp.exp(jnp.where(t_in >= s, b3 - b3[:, s:s + 1, :], NEG))
                a = jnp.sum(q3 * w * k3[:, s:s + 1, :], axis=-1, keepdims=True)
                od = od + a * v3[:, s:s + 1, :]
            o = o_mm[h] + od.reshape(n, B_DV)
            y = o * lax.rsqrt(jnp.mean(o * o, axis=-1, keepdims=True) + EPS) * gn_ref[...]
            ob_ref[0, pl.ds(r0, n), sls[h]] = (y * _silu(zg[:, sls[h]])).astype(BF16)
            st_ref[h] = st[h] * jnp.exp(b[h][n - 1:n, :]) + upd[h]
        return carry

    lax.fori_loop(0, tb // n, step, 0)

    @pl.when(c == pl.num_programs(1) - 1)
    def _():
        for h in range(B_HEADS):
            s1_ref[0, h] = st_ref[h].T


def _hgrn(z3, lb, gn, s0):
    bsz, t, _ = z3.shape
    tb = _row_tile(t, 256)
    width = B_HEADS * B_DK

    def col(off):
        return pl.BlockSpec((1, tb, width), lambda b, i: (b, i, off // width))

    return pl.pallas_call(
        functools.partial(_hgrn_kernel, tb=tb),
        grid=(bsz, t // tb),
        in_specs=[col(COL_BQ), col(COL_BF), col(COL_BI), col(COL_BG),
                  pl.BlockSpec((1, width), lambda b, i: (0, 0)),
                  pl.BlockSpec((1, B_DV), lambda b, i: (0, 0)),
                  pl.BlockSpec((1, B_HEADS, B_DK, B_DV), lambda b, i: (b, 0, 0, 0))],
        out_specs=[pl.BlockSpec((1, tb, width), lambda b, i: (b, i, 0)),
                   pl.BlockSpec((1, B_HEADS, B_DK, B_DV), lambda b, i: (b, 0, 0, 0))],
        out_shape=[jax.ShapeDtypeStruct((bsz, t, width), BF16),
                   jax.ShapeDtypeStruct((bsz, B_HEADS, B_DK, B_DV), F32)],
        scratch_shapes=[pltpu.VMEM((B_HEADS, B_DV, B_DK), F32)],
        compiler_params=_cparams("parallel", "arbitrary"),
        name="hgrn2",
    )(z3, z3, z3, z3, lb.reshape(1, width), gn.reshape(1, B_DV), s0)


CONV_PAD = 8


def _mlstm_kernel(zqk_ref, zv_ref, zo_ref, zm_ref, cw_ref, cb_ref, gb_ref, gn_ref,
                  c0_ref, n0_ref, m0_ref, cv0_ref,
                  hc_ref, c1_ref, n1_ref, m1_ref,
                  xe_ref, qk_ref, c_ref, n_ref, m_ref, *, tb, nb):
    ci = pl.program_id(1)
    keep = CONV_W - 1

    @pl.when(ci == 0)
    def _():
        for bb in range(nb):
            xe_ref[bb, CONV_PAD - keep:CONV_PAD, :] = cv0_ref[bb]
            c_ref[bb] = c0_ref[bb]
            n_ref[bb] = n0_ref[bb]
            m0_all = m0_ref[bb]
            for h in range(C_HEADS):
                m_ref[bb, h] = jnp.broadcast_to(m0_all[:, h:h + 1], (1, LANES))

    for bb in range(nb):
        xe_ref[bb, CONV_PAD:CONV_PAD + tb, :] = zqk_ref[bb]
        y = cb_ref[...]
        for jw in range(CONV_W):
            y = y + cw_ref[jw:jw + 1, :] * xe_ref[bb, CONV_PAD - keep + jw:CONV_PAD - keep + jw + tb, :]
        qk_ref[bb] = _silu(y)
        xe_ref[bb, CONV_PAD - keep:CONV_PAD, :] = zqk_ref[bb, tb - keep:tb, :]

    n = CHUNK
    rr = lax.broadcasted_iota(I32, (n, n), 0)
    cc = lax.broadcasted_iota(I32, (n, n), 1)
    causal = rr >= cc
    tril = causal.astype(F32)
    eye_k = (rr == cc).astype(BF16)
    gate_sel = (lax.broadcasted_iota(I32, (2 * C_HEADS, LANES), 1) ==
                lax.broadcasted_iota(I32, (2 * C_HEADS, LANES), 0) + MISC_CI).astype(F32)

    items = [(bb, h) for bb in range(nb) for h in range(C_HEADS)]
    nt = (((1,), (1,)), ((), ()))

    def chunk(i, carry):
        r0 = pl.multiple_of(i * n, n)
        rows = pl.ds(r0, n)
        gm, bc = [], []
        for bb in range(nb):
            g = zm_ref[bb, rows, :] + gb_ref[...]
            lf = jnp.minimum(g, 0.0) - jnp.log(1.0 + jnp.exp(-jnp.abs(g)))
            gm.append(g)
            bc.append(jnp.dot(tril, lf, precision=lax.Precision.HIGHEST, preferred_element_type=F32))
        qk = [qk_ref[bb, rows, :] for bb in range(nb)]
        vv = [zv_ref[bb, rows, :] for bb in range(nb)]
        q = [qk[bb][:, h * C_DK:(h + 1) * C_DK] for bb, h in items]
        k = [qk[bb][:, CONV_CH // 2 + h * C_DK:CONV_CH // 2 + (h + 1) * C_DK] * (C_DK ** -0.5) for bb, h in items]
        vb = [vv[bb][:, h * C_DV:(h + 1) * C_DV].astype(BF16) for bb, h in items]
        qb = [x.astype(BF16) for x in q]
        c0 = [c_ref[bb, h] for bb, h in items]
        n0 = [n_ref[bb, h:h + 1, :] for bb, h in items]
        m0 = [m_ref[bb, h][:, 0:1] for bb, h in items]
        s_raw = [lax.dot_general(qb[t], k[t].astype(BF16), nt, preferred_element_type=F32) for t in range(len(items))]
        qc = [jnp.dot(qb[t], c0[t].astype(BF16), preferred_element_type=F32) for t in range(len(items))]
        gmt = [lax.dot_general(gate_sel, x, nt, precision=lax.Precision.HIGHEST, preferred_element_type=F32)
               for x in gm]
        bct = [lax.dot_general(gate_sel, x, nt, precision=lax.Precision.HIGHEST, preferred_element_type=F32)
               for x in bc]
        m, w_i, s = [], [], []
        for t, (bb, h) in enumerate(items):
            b_col = bc[bb][:, MISC_CF + h:MISC_CF + h + 1]
            b_row = bct[bb][C_HEADS + h:C_HEADS + h + 1, :]
            ig_row = gmt[bb][h:h + 1, :]
            d = jnp.where(causal, b_col - b_row + ig_row, NEG)
            inter = b_col + m0[t]
            mm = jnp.maximum(inter, jnp.max(d, axis=1, keepdims=True))
            m.append(mm)
            w_i.append(jnp.exp(inter - mm))
            s.append(s_raw[t] * jnp.exp(d - mm))
        sv = [jnp.dot(s[t].astype(BF16), vb[t], preferred_element_type=F32) for t in range(len(items))]
        ks, w_c, m_t = [], [], []
        for t, (bb, h) in enumerate(items):
            b_col = bc[bb][:, MISC_CF + h:MISC_CF + h + 1]
            ig_col = gm[bb][:, MISC_CI + h:MISC_CI + h + 1]
            mt = m[t][n - 1:n, :]
            b_last = b_col[n - 1:n, :]
            ks.append(k[t] * jnp.exp(b_last - b_col + ig_col - mt))
            w_c.append(jnp.exp(b_last + m0[t] - mt))
            m_t.append(mt)
        kst = [lax.dot_general(eye_k, x.astype(BF16), nt, preferred_element_type=F32).astype(BF16) for x in ks]
        upd = [jnp.dot(kst[t], vb[t], preferred_element_type=F32) for t in range(len(items))]
        zo = [zo_ref[bb, rows, :] for bb in range(nb)]
        for t, (bb, h) in enumerate(items):
            num = sv[t] + w_i[t] * qc[t]
            den = jnp.sum(s[t], axis=1, keepdims=True) + w_i[t] * jnp.sum(q[t] * n0[t], axis=1, keepdims=True)
            hh = num / jnp.maximum(jnp.abs(den), jnp.exp(-m[t]))
            c_ref[bb, h] = w_c[t] * c0[t] + upd[t]
            n_ref[bb, h:h + 1, :] = w_c[t] * n0[t] + jnp.sum(ks[t], axis=0, keepdims=True)
            m_ref[bb, h] = jnp.broadcast_to(m_t[t], (1, LANES))
            yn = hh * lax.rsqrt(jnp.mean(hh * hh, axis=-1, keepdims=True) + EPS) * gn_ref[...]
            hc_ref[bb, rows, h * C_DV:(h + 1) * C_DV] = \
                (yn * _sigmoid(zo[bb][:, h * C_DV:(h + 1) * C_DV])).astype(BF16)
        return carry

    lax.fori_loop(0, tb // n, chunk, 0)

    @pl.when(ci == pl.num_programs(1) - 1)
    def _():
        for bb in range(nb):
            c1_ref[bb] = c_ref[bb]
            n1_ref[bb] = n_ref[bb]
            m1_ref[bb] = jnp.concatenate([m_ref[bb, h][:, 0:1] for h in range(C_HEADS)], axis=1)


def _mlstm(z3, conv_w, conv_b, ig_b, fg_b, gn, c0, n0, m0, cv0):
    bsz, t, _ = z3.shape
    tb = _row_tile(t, 256)
    assert tb % CHUNK == 0
    width = 512
    gate_bias = jnp.zeros((1, LANES), F32)
    gate_bias = gate_bias.at[0, MISC_CI:MISC_CI + C_HEADS].set(ig_b.astype(F32))
    gate_bias = gate_bias.at[0, MISC_CF:MISC_CF + C_HEADS].set(fg_b.astype(F32))

    nb = MLSTM_BATCH if bsz % MLSTM_BATCH == 0 else 1

    def col(off, w=width):
        return pl.BlockSpec((nb, tb, w), lambda b, i: (b, i, off // w))

    def full(shape):
        nd = len(shape)
        return pl.BlockSpec(shape, lambda b, i: (0,) * nd)

    def per_b(shape):
        nd = len(shape)
        return pl.BlockSpec((nb,) + shape, lambda b, i: (b,) + (0,) * nd)

    return pl.pallas_call(
        functools.partial(_mlstm_kernel, tb=tb, nb=nb),
        grid=(bsz // nb, t // tb),
        in_specs=[col(COL_CQK), col(COL_CV), col(COL_CO), col(COL_MISC, LANES),
                  full((CONV_W, CONV_CH)), full((1, CONV_CH)), full((1, LANES)), full((1, C_DV)),
                  per_b((C_HEADS, C_DK, C_DV)), per_b((C_HEADS, C_DK)), per_b((1, C_HEADS)),
                  per_b((CONV_W - 1, CONV_CH))],
        out_specs=[pl.BlockSpec((nb, tb, width), lambda b, i: (b, i, 0)),
                   per_b((C_HEADS, C_DK, C_DV)), per_b((C_HEADS, C_DK)), per_b((1, C_HEADS))],
        out_shape=[jax.ShapeDtypeStruct((bsz, t, width), BF16),
                   jax.ShapeDtypeStruct((bsz, C_HEADS, C_DK, C_DV), F32),
                   jax.ShapeDtypeStruct((bsz, C_HEADS, C_DK), F32),
                   jax.ShapeDtypeStruct((bsz, 1, C_HEADS), F32)],
        scratch_shapes=[pltpu.VMEM((nb, CONV_PAD + tb, CONV_CH), F32),
                        pltpu.VMEM((nb, tb, CONV_CH), F32),
                        pltpu.VMEM((nb, C_HEADS, C_DK, C_DV), F32),
                        pltpu.VMEM((nb, C_HEADS, C_DK), F32),
                        pltpu.VMEM((nb, C_HEADS, 1, LANES), F32)],
        compiler_params=_cparams("parallel", "arbitrary"),
        name="mlstm",
    )(z3, z3, z3, z3, conv_w, conv_b.reshape(1, CONV_CH), gate_bias, gn.reshape(1, C_DV),
      c0, n0, m0.reshape(bsz, 1, C_HEADS), cv0)


def _out_proj_kernel(oa_ref, ob_ref, hc_ref, x_ref, w_ref, o_ref):
    wa = w_ref[0:1024, :]
    wb = w_ref[1024:1536, :]
    wc = w_ref[1536:2048, :]
    acc = jnp.dot(oa_ref[...], wa, preferred_element_type=F32)
    acc = acc + jnp.dot(ob_ref[...], wb, preferred_element_type=F32)
    acc = acc + jnp.dot(hc_ref[...], wc, preferred_element_type=F32)
    o_ref[...] = x_ref[...] + acc


def _out_proj(oa, ob, hc, x, w):
    n, d = x.shape
    tm = _row_tile(n, 512)
    tn = d
    return pl.pallas_call(
        _out_proj_kernel,
        grid=(n // tm, d // tn),
        in_specs=[pl.BlockSpec((tm, oa.shape[1]), lambda i, j: (i, 0)),
                  pl.BlockSpec((tm, ob.shape[1]), lambda i, j: (i, 0)),
                  pl.BlockSpec((tm, hc.shape[1]), lambda i, j: (i, 0)),
                  pl.BlockSpec((tm, tn), lambda i, j: (i, j)),
                  pl.BlockSpec((w.shape[0], tn), lambda i, j: (0, j))],
        out_specs=pl.BlockSpec((tm, tn), lambda i, j: (i, j)),
        out_shape=jax.ShapeDtypeStruct((n, d), F32),
        compiler_params=_cparams("parallel", "arbitrary"),
        name="out_proj",
    )(oa, ob, hc, x, w)


def _ffn_accumulate(hn_ref, wg_ref, wu_ref, wd_ref, o_ref):
    hn = hn_ref[...]
    a = jnp.dot(hn, wg_ref[0], preferred_element_type=F32)
    b = jnp.dot(hn, wu_ref[0], preferred_element_type=F32)
    hmid = (_silu(a) * b).astype(BF16)
    o_ref[...] += jnp.dot(hmid, wd_ref[0], preferred_element_type=F32)


def _ffn_norm(x, g_ref, hn_ref):
    y = x * lax.rsqrt(jnp.mean(x * x, axis=-1, keepdims=True) + EPS) * g_ref[...]
    hn_ref[...] = y.astype(BF16)


def _ffn_kernel(te_ref, nu_ref, x_ref, g_ref, wg_ref, wu_ref, wd_ref, o_ref, hn_ref):
    @pl.when(pl.program_id(1) == 0)
    def _():
        x = x_ref[...]
        _ffn_norm(x, g_ref, hn_ref)
        o_ref[...] = x

    _ffn_accumulate(hn_ref, wg_ref, wu_ref, wd_ref, o_ref)


def _moe_ffn_kernel(te_ref, nu_ref, idx_ref, idx_next_ref, x_hbm, g_ref, wg_ref, wu_ref, wd_ref, o_ref,
                    xg_ref, hn_ref, sem, *, n_tiles, rows_per_step):
    i = pl.program_id(0)
    f = pl.program_id(1)
    tm = o_ref.shape[0]
    n_used = nu_ref[0]

    def row_copy(ids_ref, slot, r):
        return pltpu.make_async_copy(x_hbm.at[pl.ds(ids_ref[0, 0, r], 1), :],
                                     xg_ref.at[slot, pl.ds(r, 1), :], sem.at[slot])

    def start_rows(ids_ref, slot, lo, count):
        def body(r, c):
            row_copy(ids_ref, slot, lo + r).start()
            return c
        lax.fori_loop(0, count, body, 0, unroll=DMA_UNROLL)

    @pl.when((i == 0) & (f == 0) & (n_used > 0))
    def _():
        start_rows(idx_ref, 0, 0, tm)

    @pl.when((i + 1 < n_tiles) & (i + 1 < n_used) & (f < tm // rows_per_step))
    def _():
        start_rows(idx_next_ref, (i + 1) % 2, f * rows_per_step, rows_per_step)

    @pl.when(i < n_used)
    def _():
        @pl.when(f == 0)
        def _():
            slot = i % 2

            def body(r, c):
                row_copy(idx_ref, slot, r).wait()
                return c
            lax.fori_loop(0, tm, body, 0, unroll=DMA_UNROLL)
            _ffn_norm(xg_ref[slot], g_ref, hn_ref)
            o_ref[...] = jnp.zeros_like(o_ref)

        _ffn_accumulate(hn_ref, wg_ref, wu_ref, wd_ref, o_ref)

    @pl.when((i >= n_used) & (f == 0))
    def _():
        o_ref[...] = jnp.zeros_like(o_ref)


def _ffn_specs(d, tf):
    def w_in_map(i, f, te, nu):
        return (te[i], 0, jnp.where(i < nu[0], f, 0))

    def w_out_map(i, f, te, nu):
        return (te[i], jnp.where(i < nu[0], f, 0), 0)

    return [pl.BlockSpec((1, d), lambda i, f, te, nu: (0, 0)),
            pl.BlockSpec((1, d, tf), w_in_map),
            pl.BlockSpec((1, d, tf), w_in_map),
            pl.BlockSpec((1, tf, d), w_out_map)]


def _ffn_dense(x, g, wg, wu, wd):
    n, d = x.shape
    ff = wg.shape[2]
    tm = _row_tile(n, 512)
    assert ff % FFN_COLS == 0
    grid_spec = pltpu.PrefetchScalarGridSpec(
        num_scalar_prefetch=2,
        grid=(n // tm, ff // FFN_COLS),
        in_specs=[pl.BlockSpec((tm, d), lambda i, f, te, nu: (i, 0))] + _ffn_specs(d, FFN_COLS),
        out_specs=pl.BlockSpec((tm, d), lambda i, f, te, nu: (i, 0)),
        scratch_shapes=[pltpu.VMEM((tm, d), BF16)],
    )
    return pl.pallas_call(
        _ffn_kernel,
        grid_spec=grid_spec,
        out_shape=jax.ShapeDtypeStruct((n, d), F32),
        compiler_params=_cparams("parallel", "arbitrary"),
        name="swiglu_ffn",
    )(jnp.zeros((n // tm,), I32), jnp.full((1,), n // tm, I32), x, g.reshape(1, d), wg, wu, wd)


def _ffn_moe(x, src_tok, g, wg, wu, wd, tile_expert, n_used, tm):
    d = x.shape[1]
    p = src_tok.shape[0]
    ff = wg.shape[2]
    n_tiles = p // tm
    nf = ff // FFN_COLS
    assert ff % FFN_COLS == 0 and p % tm == 0
    issue_steps = 1
    while issue_steps * 2 <= min(nf, 8):
        issue_steps *= 2
    ids = src_tok.reshape(n_tiles, 1, tm)
    grid_spec = pltpu.PrefetchScalarGridSpec(
        num_scalar_prefetch=2,
        grid=(n_tiles, nf),
        in_specs=[pl.BlockSpec((1, 1, tm), lambda i, f, te, nu: (i, 0, 0), memory_space=pltpu.SMEM),
                  pl.BlockSpec((1, 1, tm), lambda i, f, te, nu: (jnp.minimum(i + 1, n_tiles - 1), 0, 0),
                               memory_space=pltpu.SMEM),
                  pl.BlockSpec(memory_space=pl.ANY)] + _ffn_specs(d, FFN_COLS),
        out_specs=pl.BlockSpec((tm, d), lambda i, f, te, nu: (i, 0)),
        scratch_shapes=[pltpu.VMEM((2, tm, d), F32), pltpu.VMEM((tm, d), BF16), pltpu.SemaphoreType.DMA((2,))],
    )
    return pl.pallas_call(
        functools.partial(_moe_ffn_kernel, n_tiles=n_tiles, rows_per_step=tm // issue_steps),
        grid_spec=grid_spec,
        out_shape=jax.ShapeDtypeStruct((p, d), F32),
        compiler_params=_cparams("arbitrary", "arbitrary"),
        name="moe_ffn",
    )(tile_expert, n_used, ids, ids, x, g.reshape(1, d), wg, wu, wd)


def _router_kernel(x_ref, g_ref, w_ref, b_ref, e_ref, p_ref):
    x = x_ref[...]
    h = x * lax.rsqrt(jnp.mean(x * x, axis=-1, keepdims=True) + EPS) * g_ref[...]
    logits = jnp.dot(h, w_ref[...], precision=lax.Precision.HIGHEST, preferred_element_type=F32) + b_ref[...]
    lane = lax.broadcasted_iota(I32, logits.shape, 1)
    logits = jnp.where(lane < N_EXPERTS, logits, -jnp.inf)
    m1 = jnp.max(logits, axis=1, keepdims=True)
    i1 = jnp.min(jnp.where(logits == m1, lane, LANES), axis=1, keepdims=True)
    rest = jnp.where(lane == i1, -jnp.inf, logits)
    m2 = jnp.max(rest, axis=1, keepdims=True)
    i2 = jnp.min(jnp.where(rest == m2, lane, LANES), axis=1, keepdims=True)
    e2 = jnp.exp(m2 - m1)
    den = 1.0 + e2
    e_ref[...] = jnp.where(lane == 0, i1, jnp.where(lane == 1, i2, 0))
    p_ref[...] = jnp.where(lane == 0, 1.0 / den, jnp.where(lane == 1, e2 / den, 0.0))


def _router(x, g, w_router, b_router):
    n, d = x.shape
    tm = _row_tile(n, 512)
    w = jnp.zeros((d, LANES), F32).at[:, :N_EXPERTS].set(w_router.astype(F32))
    b = jnp.zeros((1, LANES), F32).at[0, :N_EXPERTS].set(b_router.astype(F32))
    return pl.pallas_call(
        _router_kernel,
        grid=(n // tm,),
        in_specs=[pl.BlockSpec((tm, d), lambda i: (i, 0)),
                  pl.BlockSpec((1, d), lambda i: (0, 0)),
                  pl.BlockSpec((d, LANES), lambda i: (0, 0)),
                  pl.BlockSpec((1, LANES), lambda i: (0, 0))],
        out_specs=[pl.BlockSpec((tm, LANES), lambda i: (i, 0)),
                   pl.BlockSpec((tm, LANES), lambda i: (i, 0))],
        out_shape=[jax.ShapeDtypeStruct((n, LANES), I32),
                   jax.ShapeDtypeStruct((n, LANES), F32)],
        compiler_params=_cparams("parallel"),
        name="moe_router",
    )(x, g.reshape(1, d), w, b)


def _combine_kernel(i1_ref, i2_ref, x_ref, p_ref, yb_ref, o_ref, r1_ref, r2_ref, sem):
    rows = o_ref.shape[0]

    def row_copy(idx_ref, dst_ref, r):
        return pltpu.make_async_copy(yb_ref.at[pl.ds(idx_ref[0, 0, r], 1), :], dst_ref.at[pl.ds(r, 1), :], sem)

    def start(r, c):
        row_copy(i1_ref, r1_ref, r).start()
        row_copy(i2_ref, r2_ref, r).start()
        return c

    def wait(r, c):
        row_copy(i1_ref, r1_ref, r).wait()
        row_copy(i2_ref, r2_ref, r).wait()
        return c

    lax.fori_loop(0, rows, start, 0, unroll=DMA_UNROLL)
    lax.fori_loop(0, rows, wait, 0, unroll=DMA_UNROLL)
    p = p_ref[...]
    o_ref[...] = x_ref[...] + (r1_ref[...] * p[:, 0:1] + r2_ref[...] * p[:, 1:2])


def _combine(x, yb, i1, i2, gates):
    n, d = x.shape
    rows = _row_tile(n, GATHER_ROWS)
    row = pl.BlockSpec((rows, d), lambda i: (i, 0))
    idx = pl.BlockSpec((1, 1, rows), lambda i: (i, 0, 0), memory_space=pltpu.SMEM)
    return pl.pallas_call(
        _combine_kernel,
        grid=(n // rows,),
        in_specs=[idx, idx, row, pl.BlockSpec((rows, LANES), lambda i: (i, 0)),
                  pl.BlockSpec(memory_space=pl.ANY)],
        out_specs=row,
        out_shape=jax.ShapeDtypeStruct((n, d), F32),
        scratch_shapes=[pltpu.VMEM((rows, d), F32), pltpu.VMEM((rows, d), F32), pltpu.SemaphoreType.DMA(())],
        compiler_params=_cparams("arbitrary"),
        name="moe_combine",
    )(i1.reshape(n // rows, 1, rows), i2.reshape(n // rows, 1, rows), x, gates, yb)


def _moe(x, g, w_router, b_router, wg, wu, wd):
    n, d = x.shape
    e_out, gates = _router(x, g, w_router, b_router)
    flat_e = e_out[:, :TOP_K].reshape(n * TOP_K)
    nk = n * TOP_K
    tm = 512 if nk >= 8192 else 128
    onehot = (flat_e[:, None] == jnp.arange(N_EXPERTS, dtype=I32)[None, :]).astype(I32)
    csum = jnp.cumsum(onehot, axis=0)
    counts = csum[-1]
    rank = jnp.sum((csum - 1) * onehot, axis=1)
    padded = (counts + tm - 1) // tm * tm
    pad_end = jnp.cumsum(padded)
    pad_start = pad_end - padded
    dest = (pad_start[flat_e] + rank).astype(I32)
    n_tiles = (nk + N_EXPERTS * (tm - 1) + tm - 1) // tm
    p = n_tiles * tm
    src_tok = jnp.zeros((p,), I32).at[dest].set(jnp.arange(nk, dtype=I32) // TOP_K)
    tile_expert = jnp.minimum(jnp.searchsorted(pad_end, jnp.arange(n_tiles, dtype=I32) * tm, side='right'),
                              N_EXPERTS - 1).astype(I32)
    n_used = (pad_end[-1] // tm).astype(I32).reshape(1)

    yb = _ffn_moe(x, src_tok, g, wg, wu, wd, tile_expert, n_used, tm)
    dest2 = dest.reshape(n, TOP_K)
    return _combine(x, yb, dest2[:, 0], dest2[:, 1], gates)


def _reorder_w_in(w):
    offs, o = {}, 0
    for name, width in (("a_q", 1024), ("a_k", 256), ("a_v", 256), ("i_q", 512), ("i_w", 8), ("i_k", 64),
                        ("b_q", 512), ("b_f", 512), ("b_i", 512), ("b_g", 512), ("c_qk", 512), ("c_v", 512),
                        ("c_i", 4), ("c_f", 4), ("c_o", 512)):
        offs[name] = (o, width)
        o += width

    def c(name):
        s, width = offs[name]
        return w[:, s:s + width]

    pad = jnp.zeros((w.shape[0], LANES - (IDX_DIM + IDX_HEADS + 2 * C_HEADS)), w.dtype)
    cols = [c("a_q"), c("a_k"), c("a_v"), c("i_q"), c("b_q"), c("b_f"), c("b_i"), c("b_g"),
            c("c_qk"), c("c_v"), c("c_o"), c("i_k"), c("i_w"), c("c_i"), c("c_f"), pad]
    return jnp.concatenate(cols, axis=1).astype(BF16)


def _trunk(x, pos0, W, lbs, past):
    bsz, t, d = x.shape
    n = bsz * t
    pos = pos0 + jnp.arange(t, dtype=I32)
    xf = x.reshape(n, d)
    states = []
    depth = W['w_in'].shape[0]
    for l in range(depth):
        z = _norm_matmul(xf, W['attn_norm'][l], W['w_in_r'][l], tn=1152)
        z3 = z.reshape(bsz, t, Z_WIDTH)
        q, k, qi, ki = _prep(z3, W['q_norm'][l], W['k_norm'][l], pos)
        v = z3[:, :, COL_AV:COL_AV + 256]
        if past is None:
            n_keys = t
            assert t % KEY_BLOCK == 0
            o_a = _sparse_attention(q, qi, z3, k, z3, COL_AV // 256, ki, n_keys, 0)
            s0 = jnp.zeros((bsz, B_HEADS, B_DK, B_DV), F32)
            c0 = jnp.zeros((bsz, C_HEADS, C_DK, C_DV), F32)
            n0 = jnp.zeros((bsz, C_HEADS, C_DK), F32)
            m0 = jnp.zeros((bsz, C_HEADS), F32)
            cv0 = jnp.zeros((bsz, CONV_W - 1, CONV_CH), F32)
        else:
            pk = past['k'][l].reshape(bsz, -1, 256)
            pv = past['v'][l].reshape(bsz, -1, 256)
            n_keys = pk.shape[1] + t
            lp = (n_keys + KEY_BLOCK - 1) // KEY_BLOCK * KEY_BLOCK
            padk = jnp.zeros((bsz, lp - n_keys, 256), F32)
            k_all = jnp.concatenate([pk, k, padk], axis=1)
            v_all = jnp.concatenate([pv, v, padk], axis=1)
            ki_all = jnp.concatenate([past['ki'][l], ki, padk[:, :, :IDX_DIM]], axis=1)
            o_a = _sparse_attention(q, qi, z3, k_all, v_all, 0, ki_all, n_keys, pk.shape[1])
            s0, c0, n0, m0, cv0 = past['hgrn'][l], past['C'][l], past['n'][l], past['m'][l], past['conv'][l]
        ob, s1 = _hgrn(z3, lbs[l], W['hgrn_norm'][l], s0)
        hc, c1, n1, m1 = _mlstm(z3, W['conv_w'][l], W['conv_b'][l], W['ig_b'][l], W['fg_b'][l],
                                W['mlstm_norm'][l], c0, n0, m0, cv0)
        cqk = z3[:, :, COL_CQK:COL_CQK + CONV_CH]
        conv1 = jnp.concatenate([cv0, cqk], axis=1)[:, -(CONV_W - 1):]
        xf = _out_proj(o_a.reshape(n, -1), ob.reshape(n, -1), hc.reshape(n, -1), xf, W['w_out_b'][l])
        jj = l // 2
        if l % 2 == 0:
            xf = _ffn_dense(xf, W['ffn_norm'][l], W['dense_wg_b'][jj][None], W['dense_wu_b'][jj][None],
                            W['dense_wd_b'][jj][None])
        else:
            xf = _moe(xf, W['ffn_norm'][l], W['moe_router'][jj], W['moe_router_b'][jj],
                      W['moe_wg_b'][jj], W['moe_wu_b'][jj], W['moe_wd_b'][jj])
        states.append(dict(k=k.reshape(bsz, t, A_KV_HEADS, HEAD_DIM), v=v.reshape(bsz, t, A_KV_HEADS, HEAD_DIM),
                           ki=ki, hgrn=s1, C=c1, n=n1, m=m1.reshape(bsz, C_HEADS), conv=conv1))
    stacked = {name: jnp.stack([s[name] for s in states]) for name in states[0]}
    return xf.reshape(bsz, t, d), stacked


def kernel(x_prompt, x_sample, cache_k, cache_v, cache_kidx, state_hgrn, state_mlstm_C, state_mlstm_n,
           state_mlstm_m, state_mlstm_conv, attn_norm, w_in, w_out, q_norm, k_norm, hgrn_lb_logits,
           hgrn_norm, conv_w, conv_b, ig_b, fg_b, mlstm_norm, ffn_norm, dense_wg, dense_wu, dense_wd,
           moe_router, moe_router_b, moe_wg, moe_wu, moe_wd):
    depth = w_in.shape[0]
    W = dict(attn_norm=attn_norm, w_in=w_in, q_norm=q_norm, k_norm=k_norm, hgrn_norm=hgrn_norm,
             conv_w=conv_w, conv_b=conv_b, ig_b=ig_b, fg_b=fg_b, mlstm_norm=mlstm_norm, ffn_norm=ffn_norm,
             moe_router=moe_router, moe_router_b=moe_router_b,
             w_in_r=jnp.stack([_reorder_w_in(w_in[l]) for l in range(depth)]),
             w_out_b=w_out.astype(BF16),
             dense_wg_b=dense_wg.astype(BF16), dense_wu_b=dense_wu.astype(BF16), dense_wd_b=dense_wd.astype(BF16),
             moe_wg_b=moe_wg.astype(BF16), moe_wu_b=moe_wu.astype(BF16), moe_wd_b=moe_wd.astype(BF16))
    p_lb = jax.nn.softmax(hgrn_lb_logits.astype(F32), axis=0)
    lbs = jnp.cumsum(p_lb, axis=0) - p_lb[0]
    y_prompt, sp = _trunk(x_prompt, 0, W, lbs, None)
    past = dict(k=cache_k, v=cache_v, ki=cache_kidx, hgrn=state_hgrn, C=state_mlstm_C, n=state_mlstm_n,
                m=state_mlstm_m, conv=state_mlstm_conv)
    y_sample, ss = _trunk(x_sample, cache_k.shape[2], W, lbs, past)
    return (y_prompt, y_sample,
            sp['k'], sp['v'], sp['ki'], sp['hgrn'], sp['C'], sp['n'], sp['m'], sp['conv'],
            ss['k'], ss['v'], ss['ki'], ss['hgrn'], ss['C'], ss['n'], ss['m'], ss['conv'])
```

```python
import functools
import math

import numpy as np
import jax
import jax.numpy as jnp
from jax import lax
from jax.experimental import pallas as pl
from jax.experimental.pallas import tpu as pltpu

F32 = jnp.float32
BF16 = jnp.bfloat16
I32 = jnp.int32

CHUNK = 64
EPS = 1e-6
NEG = -1e30
HEAD_DIM = 128
A_HEADS = 8
A_KV_HEADS = 2
GQA = A_HEADS // A_KV_HEADS
IDX_HEADS = 8
IDX_DIM = 64
TOPK_MAX = 256
ROPE_THETA = 500000.0
ROT_FRAC = 4
ATTN_SCALE = HEAD_DIM ** -0.5
IDX_SCALE = (IDX_HEADS * IDX_DIM) ** -0.5
B_HEADS = 4
B_DK = 128
B_DV = 128
C_HEADS = 4
C_DK = 64
C_DV = 128
CONV_W = 4
CONV_CH = 2 * C_HEADS * C_DK
N_EXPERTS = 8
TOP_K = 2

COL_AQ, COL_AK, COL_AV, COL_IQ = 0, 1024, 1280, 1536
COL_BQ, COL_BF, COL_BI, COL_BG = 2048, 2560, 3072, 3584
COL_CQK, COL_CV, COL_CO, COL_MISC = 4096, 4608, 5120, 5632
Z_WIDTH = 5760
MISC_IK, MISC_IW, MISC_CI, MISC_CF = 0, 64, 72, 76

LANES = 128
VMEM_LIMIT_BYTES = 56 * 1024 * 1024
INT_MIN = -2 ** 31

HGRN_SUB = 8
KEY_BLOCK = 1024
ATTN_ROWS = 256
GATHER_ROWS = 256
FFN_COLS = 512
FFN_OUT_COLS = 512
DMA_UNROLL = 8
MLSTM_BATCH = 2


def _cparams(*sem):
    return pltpu.CompilerParams(dimension_semantics=sem, vmem_limit_bytes=VMEM_LIMIT_BYTES)


def _sigmoid(x):
    return 1.0 / (1.0 + jnp.exp(-x))


def _silu(x):
    return x * _sigmoid(x)


def _row_tile(n, pref):
    t = min(n, pref)
    while n % t:
        t //= 2
    return t


def _norm_matmul_kernel(x_ref, g_ref, w_ref, o_ref, xn_ref):
    @pl.when(pl.program_id(1) == 0)
    def _():
        x = x_ref[...]
        y = x * lax.rsqrt(jnp.mean(x * x, axis=-1, keepdims=True) + EPS) * g_ref[...]
        xn_ref[...] = y.astype(BF16)

    o_ref[...] = jnp.dot(xn_ref[...], w_ref[...], preferred_element_type=F32)


def _norm_matmul(x, g, w, tn):
    n, d = x.shape
    nout = w.shape[1]
    tm = _row_tile(n, 1024)
    return pl.pallas_call(
        _norm_matmul_kernel,
        grid=(n // tm, nout // tn),
        in_specs=[pl.BlockSpec((tm, d), lambda i, j: (i, 0)),
                  pl.BlockSpec((1, d), lambda i, j: (0, 0)),
                  pl.BlockSpec((d, tn), lambda i, j: (0, j))],
        out_specs=pl.BlockSpec((tm, tn), lambda i, j: (i, j)),
        out_shape=jax.ShapeDtypeStruct((n, nout), F32),
        scratch_shapes=[pltpu.VMEM((tm, d), BF16)],
        compiler_params=_cparams("parallel", "arbitrary"),
        name="norm_in_proj",
    )(x, g.reshape(1, d), w)


def _rope(x, c, sa, sb, half):
    w = x.shape[-1]
    return x * c + pltpu.roll(x, w - half, 1) * sa + pltpu.roll(x, half, 1) * sb


def _prep_kernel(zq_ref, zk_ref, ziq_ref, zm_ref, qn_ref, kn_ref, t128_ref, t64_ref,
                 q_ref, k_ref, qi_ref, ki_ref):
    c1, sa1, sb1 = t128_ref[0], t128_ref[1], t128_ref[2]
    c2, sa2, sb2 = t64_ref[0], t64_ref[1], t64_ref[2]
    half1 = HEAD_DIM // ROT_FRAC // 2
    half2 = IDX_DIM // ROT_FRAC // 2

    def normed(x, g):
        return x * lax.rsqrt(jnp.mean(x * x, axis=-1, keepdims=True) + EPS) * g

    for h in range(A_HEADS):
        x = zq_ref[0, :, h * HEAD_DIM:(h + 1) * HEAD_DIM]
        q_ref[0, h] = _rope(normed(x, qn_ref[...]), c1, sa1, sb1, half1)
    for h in range(A_KV_HEADS):
        x = zk_ref[0, :, h * HEAD_DIM:(h + 1) * HEAD_DIM]
        k_ref[0, :, h * HEAD_DIM:(h + 1) * HEAD_DIM] = _rope(normed(x, kn_ref[...]), c1, sa1, sb1, half1)
    for j in range(IDX_HEADS * IDX_DIM // LANES):
        x = ziq_ref[0, :, j * LANES:(j + 1) * LANES]
        qi_ref[0, :, j * LANES:(j + 1) * LANES] = _rope(x, c2, sa2, sb2, half2)
    m = _rope(zm_ref[0], c2, sa2, sb2, half2)
    ki_ref[0] = m[:, MISC_IK:MISC_IK + IDX_DIM]


def _rope_tables(pos, period):
    rot = period // ROT_FRAC
    half = rot // 2
    inv = jnp.exp(-math.log(ROPE_THETA) * 2.0 * jnp.arange(half, dtype=F32) / rot)
    ang = pos.astype(F32)[:, None] * inv
    cos, sin = jnp.cos(ang), jnp.sin(ang)
    t = pos.shape[0]
    ones = jnp.ones((t, period - rot), F32)
    zeros_h = jnp.zeros((t, half), F32)
    zeros_r = jnp.zeros((t, period - rot), F32)
    c = jnp.concatenate([cos, cos, ones], axis=1)
    sa = jnp.concatenate([-sin, zeros_h, zeros_r], axis=1)
    sb = jnp.concatenate([zeros_h, sin, zeros_r], axis=1)
    tab = jnp.stack([c, sa, sb])
    return jnp.tile(tab, (1, 1, LANES // period))


def _prep(z3, q_norm, k_norm, pos):
    bsz, t, _ = z3.shape
    tt = _row_tile(t, 256)
    t128 = _rope_tables(pos, HEAD_DIM)
    t64 = _rope_tables(pos, IDX_DIM)

    def col(width, off):
        return pl.BlockSpec((1, tt, width), lambda b, i: (b, i, off // width))

    return pl.pallas_call(
        _prep_kernel,
        grid=(bsz, t // tt),
        in_specs=[col(1024, COL_AQ), col(256, COL_AK), col(512, COL_IQ), col(128, COL_MISC),
                  pl.BlockSpec((1, HEAD_DIM), lambda b, i: (0, 0)),
                  pl.BlockSpec((1, HEAD_DIM), lambda b, i: (0, 0)),
                  pl.BlockSpec((3, tt, LANES), lambda b, i: (0, i, 0)),
                  pl.BlockSpec((3, tt, LANES), lambda b, i: (0, i, 0))],
        out_specs=[pl.BlockSpec((1, A_HEADS, tt, HEAD_DIM), lambda b, i: (b, 0, i, 0)),
                   pl.BlockSpec((1, tt, A_KV_HEADS * HEAD_DIM), lambda b, i: (b, i, 0)),
                   pl.BlockSpec((1, tt, IDX_HEADS * IDX_DIM), lambda b, i: (b, i, 0)),
                   pl.BlockSpec((1, tt, IDX_DIM), lambda b, i: (b, i, 0))],
        out_shape=[jax.ShapeDtypeStruct((bsz, A_HEADS, t, HEAD_DIM), F32),
                   jax.ShapeDtypeStruct((bsz, t, A_KV_HEADS * HEAD_DIM), F32),
                   jax.ShapeDtypeStruct((bsz, t, IDX_HEADS * IDX_DIM), F32),
                   jax.ShapeDtypeStruct((bsz, t, IDX_DIM), F32)],
        compiler_params=_cparams("parallel", "parallel"),
        name="qk_prep",
    )(z3, z3, z3, z3, q_norm.reshape(1, HEAD_DIM), k_norm.reshape(1, HEAD_DIM), t128, t64)


def _f2key(x):
    bits = lax.bitcast_convert_type(x, I32)
    return jnp.where(bits < 0, bits ^ 0x7FFFFFFF, bits)


def _attn_kernel(q_ref, qi_ref, zm_ref, k_ref, v_ref, ki_ref, o_ref,
                 key_ref, bias_ref, qs_ref, kb_ref, vb_ref, kib_ref, acc_ref, m_ref, l_ref, *,
                 tq, n_keys, topk, q_pos0, kb_size, lp):
    j = pl.program_id(1)
    q_first = q_pos0 + j * tq
    lim = jnp.minimum(((q_first + tq - 1) // CHUNK + 1) * CHUNK, n_keys)
    n_kb = (lim + kb_size - 1) // kb_size

    @pl.when(j == 0)
    def _():
        def cast_body(kb, c):
            sl = pl.ds(pl.multiple_of(kb * kb_size, kb_size), kb_size)
            kb_ref[sl, :] = k_ref[0, sl, :].astype(BF16)
            vb_ref[sl, :] = v_ref[0, sl, :].astype(BF16)
            kib_ref[sl, :] = ki_ref[0, sl, :].astype(BF16)
            return c
        lax.fori_loop(0, lp // kb_size, cast_body, 0)

    for h in range(A_HEADS):
        qs_ref[h // GQA, (h % GQA) * tq:(h % GQA + 1) * tq, :] = \
            (q_ref[0, h] * (ATTN_SCALE * math.log2(math.e))).astype(BF16)

    row = lax.broadcasted_iota(I32, (tq, kb_size), 0)
    lane = lax.broadcasted_iota(I32, (tq, kb_size), 1)
    q_chunk = lax.shift_right_logical(q_first + row, 6)

    qi = qi_ref[0].astype(BF16)
    wi = zm_ref[0][:, MISC_IW:MISC_IW + IDX_HEADS] * IDX_SCALE

    def score_body(kb, carry):
        k0 = pl.multiple_of(kb * kb_size, kb_size)
        ki_blk = kib_ref[pl.ds(k0, kb_size), :]
        s = jnp.zeros((tq, kb_size), F32)
        for h in range(IDX_HEADS):
            r = lax.dot_general(qi[:, h * IDX_DIM:(h + 1) * IDX_DIM], ki_blk,
                                (((1,), (1,)), ((), ())), preferred_element_type=F32)
            s = s + jnp.maximum(r, 0.0) * wi[:, h:h + 1]
        kidx = k0 + lane
        admiss = (lax.shift_right_logical(kidx, 6) <= q_chunk) & (kidx < n_keys)
        key_ref[kb] = _f2key(jnp.where(admiss, s, NEG))
        return carry

    lax.fori_loop(0, n_kb, score_body, 0)

    def count(pred_fn):
        def body(kb, acc):
            m = pred_fn(key_ref[kb], kb).astype(I32)
            for c in range(kb_size // LANES):
                acc = acc + m[:, c * LANES:(c + 1) * LANES]
            return acc
        acc = lax.fori_loop(0, n_kb, body, jnp.zeros((tq, LANES), I32))
        return jnp.sum(acc, axis=1, keepdims=True)

    q_chunk1 = lax.shift_right_logical(q_first + lax.broadcasted_iota(I32, (tq, 1), 0), 6)
    n_adm = jnp.minimum((q_chunk1 + 1) * CHUNK, n_keys)
    neg_key = int(np.float32(NEG).view(np.int32)) ^ 0x7FFFFFFF
    few = n_adm <= topk

    def search_cond(st):
        step, prefix, thr, c_thr, done = st
        return (step < 32) & (jnp.min(done) == 0)

    def search_body(st):
        step, prefix, thr, c_thr, done = st
        cand = prefix ^ lax.shift_left(jnp.int32(1), 31 - step)
        c = count(lambda key, kb: key >= cand)
        ge = c >= topk
        hit = (c == topk) & (done == 0)
        return (step + 1, jnp.where(ge, cand, prefix), jnp.where(hit, cand, thr),
                jnp.where(hit, c, c_thr), jnp.where(hit, 1, done))

    init = (jnp.int32(0), jnp.full((tq, 1), INT_MIN, I32), jnp.full((tq, 1), neg_key, I32),
            jnp.full((tq, 1), topk, I32), few.astype(I32))
    step_f, prefix_f, thr, c_ge, done = lax.while_loop(search_cond, search_body, init)
    thr = jnp.where(done != 0, thr, prefix_f)
    c_ge = jnp.where(done != 0, c_ge, topk + 1)

    def tie_limit():
        need = topk - count(lambda key, kb: key > thr)

        def idx_body(i, x):
            cand = x | lax.shift_left(jnp.int32(1), 14 - i)
            c = count(lambda key, kb: (key == thr) & (kb * kb_size + lane < cand))
            return jnp.where(c < need, cand, x)
        return lax.fori_loop(0, 15, idx_body, jnp.zeros((tq, 1), I32))

    xlim = lax.cond(jnp.max(c_ge) > topk, tie_limit, lambda: jnp.full((tq, 1), 2 ** 30, I32))

    valid_key = int(np.float32(0.5 * NEG).view(np.int32)) ^ 0x7FFFFFFF

    def mask_body(kb, carry):
        key = key_ref[kb]
        sel = (key > thr) | ((key == thr) & (kb * kb_size + lane <= xlim))
        bias_ref[kb] = jnp.where(sel & (key > valid_key), 0.0, NEG)
        return carry

    lax.fori_loop(0, n_kb, mask_body, 0)

    n_col = kb_size // LANES

    def logits(g, kb):
        sl = pl.ds(pl.multiple_of(kb * kb_size, kb_size), kb_size)
        k_blk = kb_ref[sl, g * HEAD_DIM:(g + 1) * HEAD_DIM]
        s = lax.dot_general(qs_ref[g], k_blk, (((1,), (1,)), ((), ())), preferred_element_type=F32)
        return s.reshape(GQA, tq, kb_size) + bias_ref[kb][None]

    for g in range(A_KV_HEADS):
        def max_body(kb, mx, g=g):
            s = logits(g, kb)
            for c in range(n_col):
                mx = jnp.maximum(mx, s[:, :, c * LANES:(c + 1) * LANES])
            return mx

        mx = lax.fori_loop(0, n_kb, max_body, jnp.full((GQA, tq, LANES), NEG, F32))
        m_ref[...] = jnp.broadcast_to(jnp.max(mx, axis=2, keepdims=True), (GQA, tq, LANES))
        l_ref[...] = jnp.zeros(l_ref.shape, F32)
        acc_ref[...] = jnp.zeros(acc_ref.shape, F32)

        def att_body(kb, carry, g=g):
            sl = pl.ds(pl.multiple_of(kb * kb_size, kb_size), kb_size)
            v_blk = vb_ref[sl, g * HEAD_DIM:(g + 1) * HEAD_DIM]
            mb = m_ref[...]
            p = jnp.exp2(logits(g, kb) - jnp.concatenate([mb] * n_col, axis=2))
            lsum = l_ref[...]
            for c in range(n_col):
                lsum = lsum + p[:, :, c * LANES:(c + 1) * LANES]
            l_ref[...] = lsum
            pv = jnp.dot(p.reshape(GQA * tq, kb_size).astype(BF16), v_blk, preferred_element_type=F32)
            acc_ref[...] += pv.reshape(GQA, tq, HEAD_DIM)
            return carry

        lax.fori_loop(0, n_kb, att_body, 0)
        out = acc_ref[...] / jnp.sum(l_ref[...], axis=2, keepdims=True)
        for hh in range(GQA):
            h = g * GQA + hh
            o_ref[0, :, h * HEAD_DIM:(h + 1) * HEAD_DIM] = out[hh].astype(BF16)


def _sparse_attention(q, qi, z3, k_all, v_src, v_col, ki_all, n_keys, q_pos0):
    bsz, _, t, _ = q.shape
    lp = k_all.shape[1]
    tq = _row_tile(t, ATTN_ROWS)
    topk = min(TOPK_MAX, n_keys // 4)
    kb_size = KEY_BLOCK
    assert lp % kb_size == 0 and kb_size >= topk
    kern = functools.partial(_attn_kernel, tq=tq, n_keys=n_keys, topk=topk, q_pos0=q_pos0, kb_size=kb_size,
                             lp=lp)
    return pl.pallas_call(
        kern,
        grid=(bsz, t // tq),
        in_specs=[pl.BlockSpec((1, A_HEADS, tq, HEAD_DIM), lambda b, i: (b, 0, i, 0)),
                  pl.BlockSpec((1, tq, IDX_HEADS * IDX_DIM), lambda b, i: (b, i, 0)),
                  pl.BlockSpec((1, tq, LANES), lambda b, i: (b, i, COL_MISC // LANES)),
                  pl.BlockSpec((1, lp, 256), lambda b, i: (b, 0, 0)),
                  pl.BlockSpec((1, lp, 256), lambda b, i: (b, 0, v_col)),
                  pl.BlockSpec((1, lp, IDX_DIM), lambda b, i: (b, 0, 0))],
        out_specs=pl.BlockSpec((1, tq, A_HEADS * HEAD_DIM), lambda b, i: (b, i, 0)),
        out_shape=jax.ShapeDtypeStruct((bsz, t, A_HEADS * HEAD_DIM), BF16),
        scratch_shapes=[pltpu.VMEM((lp // kb_size, tq, kb_size), I32),
                        pltpu.VMEM((lp // kb_size, tq, kb_size), F32),
                        pltpu.VMEM((A_KV_HEADS, GQA * tq, HEAD_DIM), BF16),
                        pltpu.VMEM((lp, A_KV_HEADS * HEAD_DIM), BF16),
                        pltpu.VMEM((lp, A_KV_HEADS * HEAD_DIM), BF16),
                        pltpu.VMEM((lp, IDX_DIM), BF16),
                        pltpu.VMEM((GQA, tq, HEAD_DIM), F32),
                        pltpu.VMEM((GQA, tq, LANES), F32),
                        pltpu.VMEM((GQA, tq, LANES), F32)],
        compiler_params=_cparams("parallel", "arbitrary"),
        name="sparse_attention",
    )(q, qi, z3, k_all, v_src, ki_all)


def _hgrn_kernel(zq_ref, zf_ref, zi_ref, zg_ref, lb_ref, gn_ref, s0_ref, ob_ref, s1_ref, st_ref, *, tb):
    c = pl.program_id(1)
    sub = HGRN_SUB

    @pl.when(c == 0)
    def _():
        for h in range(B_HEADS):
            st_ref[h] = s0_ref[0, h].T

    n = CHUNK
    ns = n // sub
    rr = lax.broadcasted_iota(I32, (n, n), 0)
    cc = lax.broadcasted_iota(I32, (n, n), 1)
    tril = (rr >= cc).astype(F32)
    rowi = lax.broadcasted_iota(I32, (n, B_DK), 0)
    t_in = lax.broadcasted_iota(I32, (ns, sub, B_DK), 1)
    nt = (((1,), (1,)), ((), ()))

    def step(i, carry):
        r0 = pl.multiple_of(i * n, n)
        zq = zq_ref[0, pl.ds(r0, n), :]
        zf = zf_ref[0, pl.ds(r0, n), :]
        zi = zi_ref[0, pl.ds(r0, n), :]
        zg = zg_ref[0, pl.ds(r0, n), :]
        heads = range(B_HEADS)
        sls = [slice(h * B_DK, (h + 1) * B_DK) for h in heads]
        k, q, v, vb, b = [], [], [], [], []
        for h in heads:
            lb = lb_ref[:, sls[h]]
            f = lb + (1.0 - lb) * _sigmoid(zf[:, sls[h]])
            logf = jnp.log(jnp.maximum(f, 1e-30))
            k.append(1.0 - f)
            q.append(_silu(zq[:, sls[h]]))
            v.append(zi[:, sls[h]])
            vb.append(v[h].astype(BF16))
            b.append(jnp.dot(tril, logf, precision=lax.Precision.HIGHEST, preferred_element_type=F32))
        st = [st_ref[h] for h in heads]
        o_mm = [lax.dot_general((q[h] * jnp.exp(b[h])).astype(BF16), st[h].astype(BF16), nt,
                                preferred_element_type=F32) for h in heads]
        upd = []
        for h in heads:
            strips = [jnp.zeros((sub, n), F32)]
            for j in range(1, ns):
                ref = b[h][j * sub - 1:j * sub, :]
                qt = q[h][j * sub:(j + 1) * sub] * jnp.exp(b[h][j * sub:(j + 1) * sub] - ref)
                kt = k[h] * jnp.exp(jnp.where(rowi < j * sub, ref - b[h], NEG))
                strips.append(lax.dot_general(qt.astype(BF16), kt.astype(BF16), nt,
                                              preferred_element_type=F32))
            o_mm[h] = o_mm[h] + jnp.dot(jnp.concatenate(strips, axis=0).astype(BF16), vb[h],
                                        preferred_element_type=F32)
            kt = k[h] * jnp.exp(b[h][n - 1:n, :] - b[h])
            upd.append(lax.dot_general(vb[h], kt.astype(BF16), (((0,), (0,)), ((), ())),
                                       preferred_element_type=F32))
        for h in heads:
            b3 = b[h].reshape(ns, sub, B_DK)
            q3 = q[h].reshape(ns, sub, B_DK)
            k3 = k[h].reshape(ns, sub, B_DK)
            v3 = v[h].reshape(ns, sub, B_DV)
            od = jnp.zeros((ns, sub, B_DV), F32)
            for s in range(sub):
                w = jnp.exp(jnp.where(t_in >= s, b3 - b3[:, s:s + 1, :], NEG))
                a = jnp.sum(q3 * w * k3[:, s:s + 1, :], axis=-1, keepdims=True)
                od = od + a * v3[:, s:s + 1, :]
            o = o_mm[h] + od.reshape(n, B_DV)
            y = o * lax.rsqrt(jnp.mean(o * o, axis=-1, keepdims=True) + EPS) * gn_ref[...]
            ob_ref[0, pl.ds(r0, n), sls[h]] = (y * _silu(zg[:, sls[h]])).astype(BF16)
            st_ref[h] = st[h] * jnp.exp(b[h][n - 1:n, :]) + upd[h]
        return carry

    lax.fori_loop(0, tb // n, step, 0)

    @pl.when(c == pl.num_programs(1) - 1)
    def _():
        for h in range(B_HEADS):
            s1_ref[0, h] = st_ref[h].T


def _hgrn(z3, lb, gn, s0):
    bsz, t, _ = z3.shape
    tb = _row_tile(t, 256)
    width = B_HEADS * B_DK

    def col(off):
        return pl.BlockSpec((1, tb, width), lambda b, i: (b, i, off // width))

    return pl.pallas_call(
        functools.partial(_hgrn_kernel, tb=tb),
        grid=(bsz, t // tb),
        in_specs=[col(COL_BQ), col(COL_BF), col(COL_BI), col(COL_BG),
                  pl.BlockSpec((1, width), lambda b, i: (0, 0)),
                  pl.BlockSpec((1, B_DV), lambda b, i: (0, 0)),
                  pl.BlockSpec((1, B_HEADS, B_DK, B_DV), lambda b, i: (b, 0, 0, 0))],
        out_specs=[pl.BlockSpec((1, tb, width), lambda b, i: (b, i, 0)),
                   pl.BlockSpec((1, B_HEADS, B_DK, B_DV), lambda b, i: (b, 0, 0, 0))],
        out_shape=[jax.ShapeDtypeStruct((bsz, t, width), BF16),
                   jax.ShapeDtypeStruct((bsz, B_HEADS, B_DK, B_DV), F32)],
        scratch_shapes=[pltpu.VMEM((B_HEADS, B_DV, B_DK), F32)],
        compiler_params=_cparams("parallel", "arbitrary"),
        name="hgrn2",
    )(z3, z3, z3, z3, lb.reshape(1, width), gn.reshape(1, B_DV), s0)


CONV_PAD = 8


def _mlstm_kernel(zqk_ref, zv_ref, zo_ref, zm_ref, cw_ref, cb_ref, gb_ref, gn_ref,
                  c0_ref, n0_ref, m0_ref, cv0_ref,
                  hc_ref, c1_ref, n1_ref, m1_ref,
                  xe_ref, qk_ref, c_ref, n_ref, m_ref, *, tb, nb):
    ci = pl.program_id(1)
    keep = CONV_W - 1

    @pl.when(ci == 0)
    def _():
        for bb in range(nb):
            xe_ref[bb, CONV_PAD - keep:CONV_PAD, :] = cv0_ref[bb]
            c_ref[bb] = c0_ref[bb]
            n_ref[bb] = n0_ref[bb]
            m0_all = m0_ref[bb]
            for h in range(C_HEADS):
                m_ref[bb, h] = jnp.broadcast_to(m0_all[:, h:h + 1], (1, LANES))

    for bb in range(nb):
        xe_ref[bb, CONV_PAD:CONV_PAD + tb, :] = zqk_ref[bb]
        y = cb_ref[...]
        for jw in range(CONV_W):
            y = y + cw_ref[jw:jw + 1, :] * xe_ref[bb, CONV_PAD - keep + jw:CONV_PAD - keep + jw + tb, :]
        qk_ref[bb] = _silu(y)
        xe_ref[bb, CONV_PAD - keep:CONV_PAD, :] = zqk_ref[bb, tb - keep:tb, :]

    n = CHUNK
    rr = lax.broadcasted_iota(I32, (n, n), 0)
    cc = lax.broadcasted_iota(I32, (n, n), 1)
    causal = rr >= cc
    tril = causal.astype(F32)
    eye_k = (rr == cc).astype(BF16)
    gate_sel = (lax.broadcasted_iota(I32, (2 * C_HEADS, LANES), 1) ==
                lax.broadcasted_iota(I32, (2 * C_HEADS, LANES), 0) + MISC_CI).astype(F32)

    items = [(bb, h) for bb in range(nb) for h in range(C_HEADS)]
    nt = (((1,), (1,)), ((), ()))

    def chunk(i, carry):
        r0 = pl.multiple_of(i * n, n)
        rows = pl.ds(r0, n)
        gm, bc = [], []
        for bb in range(nb):
            g = zm_ref[bb, rows, :] + gb_ref[...]
            lf = jnp.minimum(g, 0.0) - jnp.log(1.0 + jnp.exp(-jnp.abs(g)))
            gm.append(g)
            bc.append(jnp.dot(tril, lf, precision=lax.Precision.HIGHEST, preferred_element_type=F32))
        qk = [qk_ref[bb, rows, :] for bb in range(nb)]
        vv = [zv_ref[bb, rows, :] for bb in range(nb)]
        q = [qk[bb][:, h * C_DK:(h + 1) * C_DK] for bb, h in items]
        k = [qk[bb][:, CONV_CH // 2 + h * C_DK:CONV_CH // 2 + (h + 1) * C_DK] * (C_DK ** -0.5) for bb, h in items]
        vb = [vv[bb][:, h * C_DV:(h + 1) * C_DV].astype(BF16) for bb, h in items]
        qb = [x.astype(BF16) for x in q]
        c0 = [c_ref[bb, h] for bb, h in items]
        n0 = [n_ref[bb, h:h + 1, :] for bb, h in items]
        m0 = [m_ref[bb, h][:, 0:1] for bb, h in items]
        s_raw = [lax.dot_general(qb[t], k[t].astype(BF16), nt, preferred_element_type=F32) for t in range(len(items))]
        qc = [jnp.dot(qb[t], c0[t].astype(BF16), preferred_element_type=F32) for t in range(len(items))]
        gmt = [lax.dot_general(gate_sel, x, nt, precision=lax.Precision.HIGHEST, preferred_element_type=F32)
               for x in gm]
        bct = [lax.dot_general(gate_sel, x, nt, precision=lax.Precision.HIGHEST, preferred_element_type=F32)
               for x in bc]
        m, w_i, s = [], [], []
        for t, (bb, h) in enumerate(items):
            b_col = bc[bb][:, MISC_CF + h:MISC_CF + h + 1]
            b_row = bct[bb][C_HEADS + h:C_HEADS + h + 1, :]
            ig_row = gmt[bb][h:h + 1, :]
            d = jnp.where(causal, b_col - b_row + ig_row, NEG)
            inter = b_col + m0[t]
            mm = jnp.maximum(inter, jnp.max(d, axis=1, keepdims=True))
            m.append(mm)
            w_i.append(jnp.exp(inter - mm))
            s.append(s_raw[t] * jnp.exp(d - mm))
        sv = [jnp.dot(s[t].astype(BF16), vb[t], preferred_element_type=F32) for t in range(len(items))]
        ks, w_c, m_t = [], [], []
        for t, (bb, h) in enumerate(items):
            b_col = bc[bb][:, MISC_CF + h:MISC_CF + h + 1]
            ig_col = gm[bb][:, MISC_CI + h:MISC_CI + h + 1]
            mt = m[t][n - 1:n, :]
            b_last = b_col[n - 1:n, :]
            ks.append(k[t] * jnp.exp(b_last - b_col + ig_col - mt))
            w_c.append(jnp.exp(b_last + m0[t] - mt))
            m_t.append(mt)
        kst = [lax.dot_general(eye_k, x.astype(BF16), nt, preferred_element_type=F32).astype(BF16) for x in ks]
        upd = [jnp.dot(kst[t], vb[t], preferred_element_type=F32) for t in range(len(items))]
        zo = [zo_ref[bb, rows, :] for bb in range(nb)]
        for t, (bb, h) in enumerate(items):
            num = sv[t] + w_i[t] * qc[t]
            den = jnp.sum(s[t], axis=1, keepdims=True) + w_i[t] * jnp.sum(q[t] * n0[t], axis=1, keepdims=True)
            hh = num / jnp.maximum(jnp.abs(den), jnp.exp(-m[t]))
            c_ref[bb, h] = w_c[t] * c0[t] + upd[t]
            n_ref[bb, h:h + 1, :] = w_c[t] * n0[t] + jnp.sum(ks[t], axis=0, keepdims=True)
            m_ref[bb, h] = jnp.broadcast_to(m_t[t], (1, LANES))
            yn = hh * lax.rsqrt(jnp.mean(hh * hh, axis=-1, keepdims=True) + EPS) * gn_ref[...]
            hc_ref[bb, rows, h * C_DV:(h + 1) * C_DV] = \
                (yn * _sigmoid(zo[bb][:, h * C_DV:(h + 1) * C_DV])).astype(BF16)
        return carry

    lax.fori_loop(0, tb // n, chunk, 0)

    @pl.when(ci == pl.num_programs(1) - 1)
    def _():
        for bb in range(nb):
            c1_ref[bb] = c_ref[bb]
            n1_ref[bb] = n_ref[bb]
            m1_ref[bb] = jnp.concatenate([m_ref[bb, h][:, 0:1] for h in range(C_HEADS)], axis=1)


def _mlstm(z3, conv_w, conv_b, ig_b, fg_b, gn, c0, n0, m0, cv0):
    bsz, t, _ = z3.shape
    tb = _row_tile(t, 256)
    assert tb % CHUNK == 0
    width = 512
    gate_bias = jnp.zeros((1, LANES), F32)
    gate_bias = gate_bias.at[0, MISC_CI:MISC_CI + C_HEADS].set(ig_b.astype(F32))
    gate_bias = gate_bias.at[0, MISC_CF:MISC_CF + C_HEADS].set(fg_b.astype(F32))

    nb = MLSTM_BATCH if bsz % MLSTM_BATCH == 0 else 1

    def col(off, w=width):
        return pl.BlockSpec((nb, tb, w), lambda b, i: (b, i, off // w))

    def full(shape):
        nd = len(shape)
        return pl.BlockSpec(shape, lambda b, i: (0,) * nd)

    def per_b(shape):
        nd = len(shape)
        return pl.BlockSpec((nb,) + shape, lambda b, i: (b,) + (0,) * nd)

    return pl.pallas_call(
        functools.partial(_mlstm_kernel, tb=tb, nb=nb),
        grid=(bsz // nb, t // tb),
        in_specs=[col(COL_CQK), col(COL_CV), col(COL_CO), col(COL_MISC, LANES),
                  full((CONV_W, CONV_CH)), full((1, CONV_CH)), full((1, LANES)), full((1, C_DV)),
                  per_b((C_HEADS, C_DK, C_DV)), per_b((C_HEADS, C_DK)), per_b((1, C_HEADS)),
                  per_b((CONV_W - 1, CONV_CH))],
        out_specs=[pl.BlockSpec((nb, tb, width), lambda b, i: (b, i, 0)),
                   per_b((C_HEADS, C_DK, C_DV)), per_b((C_HEADS, C_DK)), per_b((1, C_HEADS))],
        out_shape=[jax.ShapeDtypeStruct((bsz, t, width), BF16),
                   jax.ShapeDtypeStruct((bsz, C_HEADS, C_DK, C_DV), F32),
                   jax.ShapeDtypeStruct((bsz, C_HEADS, C_DK), F32),
                   jax.ShapeDtypeStruct((bsz, 1, C_HEADS), F32)],
        scratch_shapes=[pltpu.VMEM((nb, CONV_PAD + tb, CONV_CH), F32),
                        pltpu.VMEM((nb, tb, CONV_CH), F32),
                        pltpu.VMEM((nb, C_HEADS, C_DK, C_DV), F32),
                        pltpu.VMEM((nb, C_HEADS, C_DK), F32),
                        pltpu.VMEM((nb, C_HEADS, 1, LANES), F32)],
        compiler_params=_cparams("parallel", "arbitrary"),
        name="mlstm",
    )(z3, z3, z3, z3, conv_w, conv_b.reshape(1, CONV_CH), gate_bias, gn.reshape(1, C_DV),
      c0, n0, m0.reshape(bsz, 1, C_HEADS), cv0)


def _out_proj_kernel(oa_ref, ob_ref, hc_ref, x_ref, w_ref, o_ref):
    wa = w_ref[0:1024, :]
    wb = w_ref[1024:1536, :]
    wc = w_ref[1536:2048, :]
    acc = jnp.dot(oa_ref[...], wa, preferred_element_type=F32)
    acc = acc + jnp.dot(ob_ref[...], wb, preferred_element_type=F32)
    acc = acc + jnp.dot(hc_ref[...], wc, preferred_element_type=F32)
    o_ref[...] = x_ref[...] + acc


def _out_proj(oa, ob, hc, x, w):
    n, d = x.shape
    tm = _row_tile(n, 512)
    tn = d
    return pl.pallas_call(
        _out_proj_kernel,
        grid=(n // tm, d // tn),
        in_specs=[pl.BlockSpec((tm, oa.shape[1]), lambda i, j: (i, 0)),
                  pl.BlockSpec((tm, ob.shape[1]), lambda i, j: (i, 0)),
                  pl.BlockSpec((tm, hc.shape[1]), lambda i, j: (i, 0)),
                  pl.BlockSpec((tm, tn), lambda i, j: (i, j)),
                  pl.BlockSpec((w.shape[0], tn), lambda i, j: (0, j))],
        out_specs=pl.BlockSpec((tm, tn), lambda i, j: (i, j)),
        out_shape=jax.ShapeDtypeStruct((n, d), F32),
        compiler_params=_cparams("parallel", "arbitrary"),
        name="out_proj",
    )(oa, ob, hc, x, w)


def _ffn_accumulate(hn_ref, wg_ref, wu_ref, wd_ref, o_ref):
    hn = hn_ref[...]
    a = jnp.dot(hn, wg_ref[0], preferred_element_type=F32)
    b = jnp.dot(hn, wu_ref[0], preferred_element_type=F32)
    hmid = (_silu(a) * b).astype(BF16)
    d = o_ref.shape[1]
    for c0 in range(0, d, FFN_OUT_COLS):
        o_ref[:, c0:c0 + FFN_OUT_COLS] += jnp.dot(hmid, wd_ref[0, :, c0:c0 + FFN_OUT_COLS],
                                                  preferred_element_type=F32)


def _ffn_norm(x, g_ref, hn_ref):
    y = x * lax.rsqrt(jnp.mean(x * x, axis=-1, keepdims=True) + EPS) * g_ref[...]
    hn_ref[...] = y.astype(BF16)


def _ffn_kernel(te_ref, nu_ref, x_ref, g_ref, wg_ref, wu_ref, wd_ref, o_ref, hn_ref):
    @pl.when(pl.program_id(1) == 0)
    def _():
        x = x_ref[...]
        _ffn_norm(x, g_ref, hn_ref)
        o_ref[...] = x

    _ffn_accumulate(hn_ref, wg_ref, wu_ref, wd_ref, o_ref)


def _moe_ffn_kernel(te_ref, nu_ref, idx_ref, idx_next_ref, x_hbm, g_ref, wg_ref, wu_ref, wd_ref, o_ref,
                    xg_ref, hn_ref, sem, *, n_tiles, rows_per_step):
    i = pl.program_id(0)
    f = pl.program_id(1)
    tm = o_ref.shape[0]
    n_used = nu_ref[0]

    def row_copy(ids_ref, slot, r):
        return pltpu.make_async_copy(x_hbm.at[pl.ds(ids_ref[0, 0, r], 1), :],
                                     xg_ref.at[slot, pl.ds(r, 1), :], sem.at[slot])

    def start_rows(ids_ref, slot, lo, count):
        def body(r, c):
            row_copy(ids_ref, slot, lo + r).start()
            return c
        lax.fori_loop(0, count, body, 0, unroll=DMA_UNROLL)

    @pl.when((i == 0) & (f == 0) & (n_used > 0))
    def _():
        start_rows(idx_ref, 0, 0, tm)

    @pl.when((i + 1 < n_tiles) & (i + 1 < n_used) & (f < tm // rows_per_step))
    def _():
        start_rows(idx_next_ref, (i + 1) % 2, f * rows_per_step, rows_per_step)

    @pl.when(i < n_used)
    def _():
        @pl.when(f == 0)
        def _():
            slot = i % 2

            def body(r, c):
                row_copy(idx_ref, slot, r).wait()
                return c
            lax.fori_loop(0, tm, body, 0, unroll=DMA_UNROLL)
            _ffn_norm(xg_ref[slot], g_ref, hn_ref)
            o_ref[...] = jnp.zeros_like(o_ref)

        _ffn_accumulate(hn_ref, wg_ref, wu_ref, wd_ref, o_ref)

    @pl.when((i >= n_used) & (f == 0))
    def _():
        o_ref[...] = jnp.zeros_like(o_ref)


def _ffn_specs(d, tf):
    def w_in_map(i, f, te, nu):
        return (te[i], 0, jnp.where(i < nu[0], f, 0))

    def w_out_map(i, f, te, nu):
        return (te[i], jnp.where(i < nu[0], f, 0), 0)

    return [pl.BlockSpec((1, d), lambda i, f, te, nu: (0, 0)),
            pl.BlockSpec((1, d, tf), w_in_map),
            pl.BlockSpec((1, d, tf), w_in_map),
            pl.BlockSpec((1, tf, d), w_out_map)]


def _ffn_dense(x, g, wg, wu, wd):
    n, d = x.shape
    ff = wg.shape[2]
    tm = _row_tile(n, 512)
    assert ff % FFN_COLS == 0
    grid_spec = pltpu.PrefetchScalarGridSpec(
        num_scalar_prefetch=2,
        grid=(n // tm, ff // FFN_COLS),
        in_specs=[pl.BlockSpec((tm, d), lambda i, f, te, nu: (i, 0))] + _ffn_specs(d, FFN_COLS),
        out_specs=pl.BlockSpec((tm, d), lambda i, f, te, nu: (i, 0)),
        scratch_shapes=[pltpu.VMEM((tm, d), BF16)],
    )
    return pl.pallas_call(
        _ffn_kernel,
        grid_spec=grid_spec,
        out_shape=jax.ShapeDtypeStruct((n, d), F32),
        compiler_params=_cparams("parallel", "arbitrary"),
        name="swiglu_ffn",
    )(jnp.zeros((n // tm,), I32), jnp.full((1,), n // tm, I32), x, g.reshape(1, d), wg, wu, wd)


def _ffn_moe(x, src_tok, g, wg, wu, wd, tile_expert, n_used, tm):
    d = x.shape[1]
    p = src_tok.shape[0]
    ff = wg.shape[2]
    n_tiles = p // tm
    nf = ff // FFN_COLS
    assert ff % FFN_COLS == 0 and p % tm == 0
    issue_steps = 1
    while issue_steps * 2 <= min(nf, 8):
        issue_steps *= 2
    ids = src_tok.reshape(n_tiles, 1, tm)
    grid_spec = pltpu.PrefetchScalarGridSpec(
        num_scalar_prefetch=2,
        grid=(n_tiles, nf),
        in_specs=[pl.BlockSpec((1, 1, tm), lambda i, f, te, nu: (i, 0, 0), memory_space=pltpu.SMEM),
                  pl.BlockSpec((1, 1, tm), lambda i, f, te, nu: (jnp.minimum(i + 1, n_tiles - 1), 0, 0),
                               memory_space=pltpu.SMEM),
                  pl.BlockSpec(memory_space=pl.ANY)] + _ffn_specs(d, FFN_COLS),
        out_specs=pl.BlockSpec((tm, d), lambda i, f, te, nu: (i, 0)),
        scratch_shapes=[pltpu.VMEM((2, tm, d), F32), pltpu.VMEM((tm, d), BF16), pltpu.SemaphoreType.DMA((2,))],
    )
    return pl.pallas_call(
        functools.partial(_moe_ffn_kernel, n_tiles=n_tiles, rows_per_step=tm // issue_steps),
        grid_spec=grid_spec,
        out_shape=jax.ShapeDtypeStruct((p, d), F32),
        compiler_params=_cparams("arbitrary", "arbitrary"),
        name="moe_ffn",
    )(tile_expert, n_used, ids, ids, x, g.reshape(1, d), wg, wu, wd)


def _router_kernel(x_ref, g_ref, w_ref, b_ref, e_ref, p_ref):
    x = x_ref[...]
    h = x * lax.rsqrt(jnp.mean(x * x, axis=-1, keepdims=True) + EPS) * g_ref[...]
    logits = jnp.dot(h, w_ref[...], precision=lax.Precision.HIGHEST, preferred_element_type=F32) + b_ref[...]
    lane = lax.broadcasted_iota(I32, logits.shape, 1)
    logits = jnp.where(lane < N_EXPERTS, logits, -jnp.inf)
    m1 = jnp.max(logits, axis=1, keepdims=True)
    i1 = jnp.min(jnp.where(logits == m1, lane, LANES), axis=1, keepdims=True)
    rest = jnp.where(lane == i1, -jnp.inf, logits)
    m2 = jnp.max(rest, axis=1, keepdims=True)
    i2 = jnp.min(jnp.where(rest == m2, lane, LANES), axis=1, keepdims=True)
    e2 = jnp.exp(m2 - m1)
    den = 1.0 + e2
    e_ref[...] = jnp.where(lane == 0, i1, jnp.where(lane == 1, i2, 0))
    p_ref[...] = jnp.where(lane == 0, 1.0 / den, jnp.where(lane == 1, e2 / den, 0.0))


def _router(x, g, w_router, b_router):
    n, d = x.shape
    tm = _row_tile(n, 512)
    w = jnp.zeros((d, LANES), F32).at[:, :N_EXPERTS].set(w_router.astype(F32))
    b = jnp.zeros((1, LANES), F32).at[0, :N_EXPERTS].set(b_router.astype(F32))
    return pl.pallas_call(
        _router_kernel,
        grid=(n // tm,),
        in_specs=[pl.BlockSpec((tm, d), lambda i: (i, 0)),
                  pl.BlockSpec((1, d), lambda i: (0, 0)),
                  pl.BlockSpec((d, LANES), lambda i: (0, 0)),
                  pl.BlockSpec((1, LANES), lambda i: (0, 0))],
        out_specs=[pl.BlockSpec((tm, LANES), lambda i: (i, 0)),
                   pl.BlockSpec((tm, LANES), lambda i: (i, 0))],
        out_shape=[jax.ShapeDtypeStruct((n, LANES), I32),
                   jax.ShapeDtypeStruct((n, LANES), F32)],
        compiler_params=_cparams("parallel"),
        name="moe_router",
    )(x, g.reshape(1, d), w, b)


def _combine_kernel(i1_ref, i2_ref, x_ref, p_ref, yb_ref, o_ref, r1_ref, r2_ref, sem):
    rows = o_ref.shape[0]

    def row_copy(idx_ref, dst_ref, r):
        return pltpu.make_async_copy(yb_ref.at[pl.ds(idx_ref[0, 0, r], 1), :], dst_ref.at[pl.ds(r, 1), :], sem)

    def start(r, c):
        row_copy(i1_ref, r1_ref, r).start()
        row_copy(i2_ref, r2_ref, r).start()
        return c

    def wait(r, c):
        row_copy(i1_ref, r1_ref, r).wait()
        row_copy(i2_ref, r2_ref, r).wait()
        return c

    lax.fori_loop(0, rows, start, 0, unroll=DMA_UNROLL)
    lax.fori_loop(0, rows, wait, 0, unroll=DMA_UNROLL)
    p = p_ref[...]
    o_ref[...] = x_ref[...] + (r1_ref[...] * p[:, 0:1] + r2_ref[...] * p[:, 1:2])


def _combine(x, yb, i1, i2, gates):
    n, d = x.shape
    rows = _row_tile(n, GATHER_ROWS)
    row = pl.BlockSpec((rows, d), lambda i: (i, 0))
    idx = pl.BlockSpec((1, 1, rows), lambda i: (i, 0, 0), memory_space=pltpu.SMEM)
    return pl.pallas_call(
        _combine_kernel,
        grid=(n // rows,),
        in_specs=[idx, idx, row, pl.BlockSpec((rows, LANES), lambda i: (i, 0)),
                  pl.BlockSpec(memory_space=pl.ANY)],
        out_specs=row,
        out_shape=jax.ShapeDtypeStruct((n, d), F32),
        scratch_shapes=[pltpu.VMEM((rows, d), F32), pltpu.VMEM((rows, d), F32), pltpu.SemaphoreType.DMA(())],
        compiler_params=_cparams("arbitrary"),
        name="moe_combine",
    )(i1.reshape(n // rows, 1, rows), i2.reshape(n // rows, 1, rows), x, gates, yb)


def _moe(x, g, w_router, b_router, wg, wu, wd):
    n, d = x.shape
    e_out, gates = _router(x, g, w_router, b_router)
    flat_e = e_out[:, :TOP_K].reshape(n * TOP_K)
    nk = n * TOP_K
    tm = 512 if nk >= 8192 else 128
    onehot = (flat_e[:, None] == jnp.arange(N_EXPERTS, dtype=I32)[None, :]).astype(I32)
    csum = jnp.cumsum(onehot, axis=0)
    counts = csum[-1]
    rank = jnp.sum((csum - 1) * onehot, axis=1)
    padded = (counts + tm - 1) // tm * tm
    pad_end = jnp.cumsum(padded)
    pad_start = pad_end - padded
    dest = (pad_start[flat_e] + rank).astype(I32)
    n_tiles = (nk + N_EXPERTS * (tm - 1) + tm - 1) // tm
    p = n_tiles * tm
    src_tok = jnp.zeros((p,), I32).at[dest].set(jnp.arange(nk, dtype=I32) // TOP_K)
    tile_expert = jnp.minimum(jnp.searchsorted(pad_end, jnp.arange(n_tiles, dtype=I32) * tm, side='right'),
                              N_EXPERTS - 1).astype(I32)
    n_used = (pad_end[-1] // tm).astype(I32).reshape(1)

    yb = _ffn_moe(x, src_tok, g, wg, wu, wd, tile_expert, n_used, tm)
    dest2 = dest.reshape(n, TOP_K)
    return _combine(x, yb, dest2[:, 0], dest2[:, 1], gates)


def _reorder_w_in(w):
    offs, o = {}, 0
    for name, width in (("a_q", 1024), ("a_k", 256), ("a_v", 256), ("i_q", 512), ("i_w", 8), ("i_k", 64),
                        ("b_q", 512), ("b_f", 512), ("b_i", 512), ("b_g", 512), ("c_qk", 512), ("c_v", 512),
                        ("c_i", 4), ("c_f", 4), ("c_o", 512)):
        offs[name] = (o, width)
        o += width

    def c(name):
        s, width = offs[name]
        return w[:, s:s + width]

    pad = jnp.zeros((w.shape[0], LANES - (IDX_DIM + IDX_HEADS + 2 * C_HEADS)), w.dtype)
    cols = [c("a_q"), c("a_k"), c("a_v"), c("i_q"), c("b_q"), c("b_f"), c("b_i"), c("b_g"),
            c("c_qk"), c("c_v"), c("c_o"), c("i_k"), c("i_w"), c("c_i"), c("c_f"), pad]
    return jnp.concatenate(cols, axis=1).astype(BF16)


def _trunk(x, pos0, W, lbs, past):
    bsz, t, d = x.shape
    n = bsz * t
    pos = pos0 + jnp.arange(t, dtype=I32)
    xf = x.reshape(n, d)
    states = []
    depth = W['w_in'].shape[0]
    for l in range(depth):
        z = _norm_matmul(xf, W['attn_norm'][l], W['w_in_r'][l], tn=1152)
        z3 = z.reshape(bsz, t, Z_WIDTH)
        q, k, qi, ki = _prep(z3, W['q_norm'][l], W['k_norm'][l], pos)
        v = z3[:, :, COL_AV:COL_AV + 256]
        if past is None:
            n_keys = t
            assert t % KEY_BLOCK == 0
            o_a = _sparse_attention(q, qi, z3, k, z3, COL_AV // 256, ki, n_keys, 0)
            s0 = jnp.zeros((bsz, B_HEADS, B_DK, B_DV), F32)
            c0 = jnp.zeros((bsz, C_HEADS, C_DK, C_DV), F32)
            n0 = jnp.zeros((bsz, C_HEADS, C_DK), F32)
            m0 = jnp.zeros((bsz, C_HEADS), F32)
            cv0 = jnp.zeros((bsz, CONV_W - 1, CONV_CH), F32)
        else:
            pk = past['k'][l].reshape(bsz, -1, 256)
            pv = past['v'][l].reshape(bsz, -1, 256)
            n_keys = pk.shape[1] + t
            lp = (n_keys + KEY_BLOCK - 1) // KEY_BLOCK * KEY_BLOCK
            padk = jnp.zeros((bsz, lp - n_keys, 256), F32)
            k_all = jnp.concatenate([pk, k, padk], axis=1)
            v_all = jnp.concatenate([pv, v, padk], axis=1)
            ki_all = jnp.concatenate([past['ki'][l], ki, padk[:, :, :IDX_DIM]], axis=1)
            o_a = _sparse_attention(q, qi, z3, k_all, v_all, 0, ki_all, n_keys, pk.shape[1])
            s0, c0, n0, m0, cv0 = past['hgrn'][l], past['C'][l], past['n'][l], past['m'][l], past['conv'][l]
        ob, s1 = _hgrn(z3, lbs[l], W['hgrn_norm'][l], s0)
        hc, c1, n1, m1 = _mlstm(z3, W['conv_w'][l], W['conv_b'][l], W['ig_b'][l], W['fg_b'][l],
                                W['mlstm_norm'][l], c0, n0, m0, cv0)
        cqk = z3[:, :, COL_CQK:COL_CQK + CONV_CH]
        conv1 = jnp.concatenate([cv0, cqk], axis=1)[:, -(CONV_W - 1):]
        xf = _out_proj(o_a.reshape(n, -1), ob.reshape(n, -1), hc.reshape(n, -1), xf, W['w_out_b'][l])
        jj = l // 2
        if l % 2 == 0:
            xf = _ffn_dense(xf, W['ffn_norm'][l], W['dense_wg_b'][jj][None], W['dense_wu_b'][jj][None],
                            W['dense_wd_b'][jj][None])
        else:
            xf = _moe(xf, W['ffn_norm'][l], W['moe_router'][jj], W['moe_router_b'][jj],
                      W['moe_wg_b'][jj], W['moe_wu_b'][jj], W['moe_wd_b'][jj])
        states.append(dict(k=k.reshape(bsz, t, A_KV_HEADS, HEAD_DIM), v=v.reshape(bsz, t, A_KV_HEADS, HEAD_DIM),
                           ki=ki, hgrn=s1, C=c1, n=n1, m=m1.reshape(bsz, C_HEADS), conv=conv1))
    stacked = {name: jnp.stack([s[name] for s in states]) for name in states[0]}
    return xf.reshape(bsz, t, d), stacked


def kernel(x_prompt, x_sample, cache_k, cache_v, cache_kidx, state_hgrn, state_mlstm_C, state_mlstm_n,
           state_mlstm_m, state_mlstm_conv, attn_norm, w_in, w_out, q_norm, k_norm, hgrn_lb_logits,
           hgrn_norm, conv_w, conv_b, ig_b, fg_b, mlstm_norm, ffn_norm, dense_wg, dense_wu, dense_wd,
           moe_router, moe_router_b, moe_wg, moe_wu, moe_wd):
    depth = w_in.shape[0]
    W = dict(attn_norm=attn_norm, w_in=w_in, q_norm=q_norm, k_norm=k_norm, hgrn_norm=hgrn_norm,
             conv_w=conv_w, conv_b=conv_b, ig_b=ig_b, fg_b=fg_b, mlstm_norm=mlstm_norm, ffn_norm=ffn_norm,
             moe_router=moe_router, moe_router_b=moe_router_b,
             w_in_r=jnp.stack([_reorder_w_in(w_in[l]) for l in range(depth)]),
             w_out_b=w_out.astype(BF16),
             dense_wg_b=dense_wg.astype(BF16), dense_wu_b=dense_wu.astype(BF16), dense_wd_b=dense_wd.astype(BF16),
             moe_wg_b=moe_wg.astype(BF16), moe_wu_b=moe_wu.astype(BF16), moe_wd_b=moe_wd.astype(BF16))
    p_lb = jax.nn.softmax(hgrn_lb_logits.astype(F32), axis=0)
    lbs = jnp.cumsum(p_lb, axis=0) - p_lb[0]
    y_prompt, sp = _trunk(x_prompt, 0, W, lbs, None)
    past = dict(k=cache_k, v=cache_v, ki=cache_kidx, hgrn=state_hgrn, C=state_mlstm_C, n=state_mlstm_n,
                m=state_mlstm_m, conv=state_mlstm_conv)
    y_sample, ss = _trunk(x_sample, cache_k.shape[2], W, lbs, past)
    return (y_prompt, y_sample,
            sp['k'], sp['v'], sp['ki'], sp['hgrn'], sp['C'], sp['n'], sp['m'], sp['conv'],
            ss['k'], ss['v'], ss['ki'], ss['hgrn'], ss['C'], ss['n'], ss['m'], ss['conv'])
```
